```python
import jax, jax.numpy as jnp
from jax import lax
import numpy as np

D_MODEL = 4096
BATCH = 2
SEQ = 8192
DEPTH = 2

MIX_WIDTH = D_MODEL // 4
N_BRANCH = 3
HEAD_DIM = 128
CHUNK = 128
A_GROUP_DIM = 128
A_GROUPS = MIX_WIDTH // A_GROUP_DIM
B_PATTERNS = ((128, 1), (512, 4), (2048, 16))
B_HEADS_PER_GROUP = MIX_WIDTH // HEAD_DIM
B_HEADS = B_HEADS_PER_GROUP * len(B_PATTERNS)
ATTN_BLOCK = 128
ROT_DIM = HEAD_DIM // 4
ROPE_THETA = 500000.0
CONV_WIDTH = 3
A_COLS = 2 * MIX_WIDTH
B_COLS = 3 * B_HEADS * HEAD_DIM
C_COLS = 3 * MIX_WIDTH
IN_COLS = A_COLS + B_COLS + C_COLS
PEER_HEADS = 8
N_KEYS = 128
N_EXPERTS = N_KEYS * N_KEYS
PEER_TOPK = 16
D_KEY = 128
PEER_TOK_BLOCK = 32
ALPHA = (2 * DEPTH) ** 0.25
BETA = (8 * DEPTH) ** -0.25
LN_EPS = 1e-5
ADA_INIT = 0.25
POS_OFFSET_MAX = 1024

kernel_name = "hybrid_gmlp_dilattn_shortconv_peer_deepnorm"


def layer_norm(x, g, b):
    xf = x.astype(jnp.float32)
    mu = jnp.mean(xf, axis=-1, keepdims=True)
    var = jnp.mean(jnp.square(xf - mu), axis=-1, keepdims=True)
    return ((xf - mu) * lax.rsqrt(var + LN_EPS)).astype(x.dtype) * g + b


def partial_rope(t, positions):
    half = ROT_DIM // 2
    inv_freq = ROPE_THETA ** (-jnp.arange(half, dtype=jnp.float32) / half)
    ang = positions.astype(jnp.float32)[..., None] * inv_freq
    cos = jnp.cos(ang)[:, :, None, :].astype(t.dtype)
    sin = jnp.sin(ang)[:, :, None, :].astype(t.dtype)
    x1 = t[..., :half]
    x2 = t[..., half:ROT_DIM]
    return jnp.concatenate([x1 * cos - x2 * sin, x2 * cos + x1 * sin, t[..., ROT_DIM:]], axis=-1)


def chunked_gmlp(uv, ln_g, ln_b, w_sp, b_sp):
    Bsz, S, _ = uv.shape
    u, v = jnp.split(uv, 2, axis=-1)
    v = layer_norm(v, ln_g, ln_b)
    v = v.reshape(Bsz, S // CHUNK, CHUNK, A_GROUPS, A_GROUP_DIM)
    causal = jnp.tril(jnp.ones((CHUNK, CHUNK), dtype=bool))
    w = jnp.where(causal, w_sp, 0)
    mixed = jnp.einsum("gts,bnsgc->bntgc", w, v) + b_sp.T[:, :, None]
    return u * mixed.reshape(Bsz, S, MIX_WIDTH)


def dilated_window_attention(q, k, v, window, dilation):
    Bsz, S, H, Dh = q.shape
    span = window // dilation
    period = ATTN_BLOCK * dilation
    Sp = -(-S // period) * period
    nb = Sp // period

    def blocks(t):
        t = jnp.pad(t, ((0, 0), (0, Sp - S), (0, 0), (0, 0)))
        return t.reshape(Bsz, nb, ATTN_BLOCK, dilation, H, Dh)

    def with_prev(t):
        prev = jnp.pad(t[:, :-1], ((0, 0), (1, 0), (0, 0), (0, 0), (0, 0), (0, 0)))
        return jnp.concatenate([prev, t], axis=2)

    qb = blocks(q)
    kk = with_prev(blocks(k))
    vv = with_prev(blocks(v))
    s = jnp.einsum("bnqrhd,bnkrhd->bnrhqk", qb, kk).astype(jnp.float32) * (Dh ** -0.5)
    qi = jnp.arange(ATTN_BLOCK)[:, None]
    kj = jnp.arange(2 * ATTN_BLOCK)[None, :]
    dist = qi + ATTN_BLOCK - kj
    band = (dist >= 0) & (dist <= span)
    exists = (jnp.arange(nb)[:, None, None] > 0) | (kj >= ATTN_BLOCK)[None]
    mask = (band[None] & exists)[None, :, None, None]
    s = jnp.where(mask, s, -jnp.inf)
    lse = jax.nn.logsumexp(s, axis=-1)
    p = jnp.exp(s - lse[..., None]).astype(v.dtype)
    o = jnp.einsum("bnrhqk,bnkrhd->bnqrhd", p, vv).reshape(Bsz, Sp, H, Dh)[:, :S]
    lse = jnp.transpose(lse, (0, 1, 4, 2, 3)).reshape(Bsz, Sp, H)[:, :S]
    return o, lse


def dilated_mixture_attention(qkv, positions):
    Bsz, S, _ = qkv.shape
    qkv = qkv.reshape(Bsz, S, 3, B_HEADS, HEAD_DIM)
    q = partial_rope(qkv[:, :, 0], positions)
    k = partial_rope(qkv[:, :, 1], positions)
    v = qkv[:, :, 2]
    outs, lses = [], []
    for g, (window, dilation) in enumerate(B_PATTERNS):
        hs = slice(g * B_HEADS_PER_GROUP, (g + 1) * B_HEADS_PER_GROUP)
        o, l = dilated_window_attention(q[:, :, hs], k[:, :, hs], v[:, :, hs], window, dilation)
        outs.append(o)
        lses.append(l)
    o = jnp.stack(outs, axis=2)
    w = jax.nn.softmax(jnp.stack(lses, axis=2), axis=2)
    out = jnp.einsum("bsgh,bsghd->bshd", w.astype(o.dtype), o)
    return out.reshape(Bsz, S, MIX_WIDTH)


def short_conv_mixer(bcx, conv_w):
    gate_b, gate_c, xin = jnp.split(bcx, 3, axis=-1)
    z = gate_c * xin
    y = lax.conv_general_dilated(
        z, conv_w[:, None, :], window_strides=(1,), padding=[(CONV_WIDTH - 1, 0)],
        dimension_numbers=("NWC", "WIO", "NWC"), feature_group_count=MIX_WIDTH)
    return gate_b * y


def mixer_sublayer(h, positions, w_in, w_gate, b_gate, ln_v_g, ln_v_b, w_sp, b_sp,
                   conv_w, w_branch, w_o):
    proj = h @ w_in
    a_uv, b_qkv, c_bcx = jnp.split(proj, [A_COLS, A_COLS + B_COLS], axis=-1)
    branches = (
        chunked_gmlp(jax.nn.gelu(a_uv), ln_v_g, ln_v_b, w_sp, b_sp),
        dilated_mixture_attention(b_qkv, positions),
        short_conv_mixer(c_bcx, conv_w),
    )
    merged = jnp.zeros_like(h)
    for g in range(N_BRANCH):
        gate = jax.nn.sigmoid(h @ w_gate[g] + b_gate[g])
        merged = merged + gate * (branches[g] @ w_branch[g])
    return merged @ w_o


def peer(h, w_pq, sub_keys, w_u, w_v):
    Bsz, S, D = h.shape
    q = (h @ w_pq).reshape(Bsz, S, PEER_HEADS, 2, D_KEY // 2)
    s = jnp.einsum("bshpd,hpnd->bshpn", q, sub_keys).astype(jnp.float32)
    s_top, i_top = lax.top_k(s, PEER_TOPK)
    cand_s = (s_top[..., 0, :, None] + s_top[..., 1, None, :]).reshape(Bsz, S, PEER_HEADS, PEER_TOPK * PEER_TOPK)
    cand_i = (i_top[..., 0, :, None] * N_KEYS + i_top[..., 1, None, :]).reshape(Bsz, S, PEER_HEADS, PEER_TOPK * PEER_TOPK)
    best_s, best_pos = lax.top_k(cand_s, PEER_TOPK)
    idx = jnp.take_along_axis(cand_i, best_pos, axis=-1)
    gate = jax.nn.softmax(best_s, axis=-1).astype(h.dtype)
    nblk = (Bsz * S) // PEER_TOK_BLOCK
    hb = h.reshape(nblk, PEER_TOK_BLOCK, D)
    ib = idx.reshape(nblk, PEER_TOK_BLOCK, PEER_HEADS * PEER_TOPK)
    gb = gate.reshape(nblk, PEER_TOK_BLOCK, PEER_HEADS * PEER_TOPK)

    def expert_block(args):
        xt, it, gt = args
        act = jax.nn.gelu(jnp.einsum("td,tkd->tk", xt, w_u[it]))
        return jnp.einsum("tk,tkd->td", gt * act, w_v[it])

    return lax.map(expert_block, (hb, ib, gb)).reshape(Bsz, S, D)


def setup_inputs(seed: int = 0) -> dict:
    key = jax.random.key(seed)
    ks = iter(jax.random.split(key, 32))
    L, D = DEPTH, D_MODEL

    def nrm(shape, scale):
        return jax.random.normal(next(ks), shape, jnp.float32) * scale

    x = nrm((BATCH, SEQ, D), 1.0)
    c = nrm((BATCH, D), 1.0)
    start = jax.random.randint(next(ks), (BATCH, 1), 0, POS_OFFSET_MAX, dtype=jnp.int32)
    positions = start + jnp.arange(SEQ, dtype=jnp.int32)[None, :]
    return {
        "x": x,
        "c": c,
        "positions": positions,
        "w_ada": nrm((L, D, 6 * D), ADA_INIT * D ** -0.5),
        "b_ada": nrm((L, 6 * D), 0.01),
        "w_in": nrm((L, D, IN_COLS), D ** -0.5),
        "w_gate": nrm((L, N_BRANCH, D, D), D ** -0.5),
        "b_gate": nrm((L, N_BRANCH, D), 0.01),
        "ln_v_g": 1.0 + nrm((L, MIX_WIDTH), 0.02),
        "ln_v_b": nrm((L, MIX_WIDTH), 0.02),
        "w_sp": nrm((L, A_GROUPS, CHUNK, CHUNK), CHUNK ** -0.5),
        "b_sp": 1.0 + nrm((L, A_GROUPS, CHUNK), 0.1),
        "conv_w": nrm((L, CONV_WIDTH, MIX_WIDTH), CONV_WIDTH ** -0.5),
        "w_branch": nrm((L, N_BRANCH, MIX_WIDTH, D), MIX_WIDTH ** -0.5),
        "w_o": nrm((L, D, D), BETA * D ** -0.5),
        "ln1_g": 1.0 + nrm((L, D), 0.02),
        "ln1_b": nrm((L, D), 0.02),
        "w_pq": nrm((L, D, PEER_HEADS * D_KEY), D ** -0.5),
        "sub_keys": nrm((L, PEER_HEADS, 2, N_KEYS, D_KEY // 2), (D_KEY // 2) ** -0.5),
        "w_u": nrm((L, N_EXPERTS, D), D ** -0.5),
        "w_v": nrm((L, N_EXPERTS, D), BETA),
        "ln2_g": 1.0 + nrm((L, D), 0.02),
        "ln2_b": nrm((L, D), 0.02),
    }


def reference(x, c, positions, w_ada, b_ada, w_in, w_gate, b_gate, ln_v_g, ln_v_b, w_sp, b_sp,
              conv_w, w_branch, w_o, ln1_g, ln1_b, w_pq, sub_keys, w_u, w_v, ln2_g, ln2_b):
    c_act = jax.nn.silu(c)
    for l in range(DEPTH):
        mod = c_act @ w_ada[l] + b_ada[l]
        sh1, sc1, g1, sh2, sc2, g2 = [m[:, None, :] for m in jnp.split(mod, 6, axis=-1)]
        h = x * (1 + sc1) + sh1
        y = mixer_sublayer(h, positions, w_in[l], w_gate[l], b_gate[l], ln_v_g[l], ln_v_b[l],
                           w_sp[l], b_sp[l], conv_w[l], w_branch[l], w_o[l])
        x = layer_norm(ALPHA * x + g1 * y, ln1_g[l], ln1_b[l])
        h = x * (1 + sc2) + sh2
        y = peer(h, w_pq[l], sub_keys[l], w_u[l], w_v[l])
        x = layer_norm(ALPHA * x + g2 * y, ln2_g[l], ln2_b[l])
    return x
```

```python
import functools

import jax
import jax.numpy as jnp
from jax import lax
from jax.experimental import pallas as pl
from jax.experimental.pallas import tpu as pltpu

F32 = jnp.float32
BF16 = jnp.bfloat16

HEAD_DIM = 128
CHUNK = 128
ATTN_BLOCK = 128
ROT_DIM = HEAD_DIM // 4
ROPE_THETA = 500000.0
B_PATTERNS = ((128, 1), (512, 4), (2048, 16))
N_BRANCH = 3
PEER_TOPK = 16
LN_EPS = 1e-5
MASKED = -1e30

V7X_LANES = 128
V7X_BF16_SUBLANES = 16
V7X_VMEM_LIMIT_CAP = 58 * 2**20
SPILL_AND_TEMP_BYTES = 4 * 2**20


def _cparams(semantics, vmem_bytes):
    limit = max(vmem_bytes + SPILL_AND_TEMP_BYTES, 16 * 2**20)
    return pltpu.CompilerParams(
        dimension_semantics=semantics,
        vmem_limit_bytes=int(min(limit, V7X_VMEM_LIMIT_CAP)),
    )


def _nbytes(shape, dtype):
    n = 1
    for s in shape:
        n *= s
    return n * jnp.dtype(dtype).itemsize


def _layer_norm_rows(z, g, b):
    mu = jnp.mean(z, axis=-1, keepdims=True)
    zc = z - mu
    var = jnp.mean(zc * zc, axis=-1, keepdims=True)
    return zc * lax.rsqrt(var + LN_EPS) * g + b


_NT = (((1,), (1,)), ((), ()))


def _rope_table_kernel(pos_ref, freq_ref, sign_ref, cos_ref, sin_ref):
    ang = pos_ref[...].astype(F32) * freq_ref[...]
    cos_ref[...] = jnp.cos(ang)
    sin_ref[...] = jnp.sin(ang) * sign_ref[...]


def _rope_tables(positions):
    T = positions.size
    half = ROT_DIM // 2
    inv_freq = ROPE_THETA ** (-jnp.arange(half, dtype=F32) / half)
    zeros = jnp.zeros((HEAD_DIM - ROT_DIM,), F32)
    freq = jnp.concatenate([inv_freq, inv_freq, zeros])[None, :]
    sign = jnp.concatenate([-jnp.ones((half,), F32), jnp.ones((half,), F32), zeros])[None, :]
    tm = min(1024, T)
    row = pl.BlockSpec((1, HEAD_DIM), lambda i: (0, 0))
    tab = pl.BlockSpec((tm, HEAD_DIM), lambda i: (i, 0))
    return pl.pallas_call(
        _rope_table_kernel,
        out_shape=(jax.ShapeDtypeStruct((T, HEAD_DIM), F32),) * 2,
        grid=(T // tm,),
        in_specs=[pl.BlockSpec((tm, 1), lambda i: (i, 0)), row, row],
        out_specs=(tab, tab),
        compiler_params=_cparams(("parallel",), 8 * _nbytes((tm, HEAD_DIM), F32)),
        name="rope_table",
    )(positions.reshape(T, 1), freq, sign)


def _adaln_kernel(c_ref, w_ref, b_ref, o_ref):
    c = c_ref[...]
    o_ref[...] = jnp.dot(c * jax.nn.sigmoid(c), w_ref[...], preferred_element_type=F32) + b_ref[...]


def _adaln(c, w_ada, b_ada):
    L, D, N = w_ada.shape
    B = c.shape[0]
    rows = 8
    c_pad = jnp.zeros((rows, D), F32).at[:B].set(c)
    tn = min(512, N)
    mod = pl.pallas_call(
        _adaln_kernel,
        out_shape=jax.ShapeDtypeStruct((L, rows, N), F32),
        grid=(L, N // tn),
        in_specs=[
            pl.BlockSpec((rows, D), lambda l, j: (0, 0)),
            pl.BlockSpec((None, D, tn), lambda l, j: (l, 0, j)),
            pl.BlockSpec((None, 1, tn), lambda l, j: (l, 0, j)),
        ],
        out_specs=pl.BlockSpec((None, rows, tn), lambda l, j: (l, 0, j)),
        compiler_params=_cparams(("parallel", "parallel"), 2 * _nbytes((D, tn), F32) + 2**22),
        name="adaln",
    )(c_pad, w_ada, b_ada.reshape(L, 1, N))
    return mod[:, :B].reshape(L * B * 6, 1, D)


def _modulate_kernel(x_ref, sc_ref, sh_ref, o_ref):
    o_ref[...] = (x_ref[...] * (1.0 + sc_ref[...]) + sh_ref[...]).astype(o_ref.dtype)


def _modulate(x, modr, sc_row, sh_row):
    B, S, D = x.shape
    ts = min(512, S)
    nst = S // ts
    return pl.pallas_call(
        _modulate_kernel,
        out_shape=jax.ShapeDtypeStruct((B * S, D), BF16),
        grid=(B, nst),
        in_specs=[
            pl.BlockSpec((None, ts, D), lambda b, s: (b, s, 0)),
            pl.BlockSpec((None, 1, D), lambda b, s: (sc_row(b), 0, 0)),
            pl.BlockSpec((None, 1, D), lambda b, s: (sh_row(b), 0, 0)),
        ],
        out_specs=pl.BlockSpec((ts, D), lambda b, s: (b * nst + s, 0)),
        compiler_params=_cparams(("parallel", "parallel"), 3 * _nbytes((ts, D), F32) + 2**22),
        name="modulate",
    )(x, modr, modr)


def _proj_in_kernel(a_ref, w_ref, cos_ref, sin_ref, o_ref, *, n_gelu, n_rope):
    j = pl.program_id(1)
    acc = jnp.dot(a_ref[...], w_ref[...], preferred_element_type=F32)
    tn = acc.shape[1]

    @pl.when(j < n_gelu)
    def _():
        o_ref[...] = jax.nn.gelu(acc).astype(o_ref.dtype)

    @pl.when((j >= n_gelu) & (j < n_gelu + n_rope))
    def _():
        reps = tn // HEAD_DIM
        cos = jnp.concatenate([cos_ref[...]] * reps, axis=1)
        sin = jnp.concatenate([sin_ref[...]] * reps, axis=1)
        lane = lax.broadcasted_iota(jnp.int32, acc.shape, 1) % HEAD_DIM
        half = ROT_DIM // 2
        partner = jnp.where(lane < half, pltpu.roll(acc, tn - half, 1), pltpu.roll(acc, half, 1))
        o_ref[...] = (acc * cos + partner * sin).astype(o_ref.dtype)

    @pl.when(j >= n_gelu + n_rope)
    def _():
        o_ref[...] = acc.astype(o_ref.dtype)


def _proj_in(h, w_in, cos_t, sin_t, mix):
    T, D = h.shape
    N = w_in.shape[1]
    tm = min(1024, T)
    tn = min(512, mix)
    n_gelu = 2 * mix // tn
    n_rope = 6 * mix // tn
    tab = pl.BlockSpec((tm, HEAD_DIM), lambda i, j: (i, 0))
    need = 2 * (_nbytes((tm, D), BF16) + _nbytes((D, tn), BF16) + _nbytes((tm, tn), BF16)) \
        + 4 * _nbytes((tm, HEAD_DIM), F32) + 6 * _nbytes((tm, tn), F32)
    return pl.pallas_call(
        functools.partial(_proj_in_kernel, n_gelu=n_gelu, n_rope=n_rope),
        out_shape=jax.ShapeDtypeStruct((T, N), BF16),
        grid=(T // tm, N // tn),
        in_specs=[
            pl.BlockSpec((tm, D), lambda i, j: (i, 0)),
            pl.BlockSpec((D, tn), lambda i, j: (0, j)),
            tab, tab,
        ],
        out_specs=pl.BlockSpec((tm, tn), lambda i, j: (i, j)),
        compiler_params=_cparams(("parallel", "arbitrary"), need),
        name="proj_in",
    )(h, w_in, cos_t, sin_t)


def _gmlp_kernel(u_ref, v_ref, g_ref, b_ref, wsp_ref, bspt_ref, o_ref, *, groups):
    tg = u_ref.shape[0]
    vn = _layer_norm_rows(v_ref[...].astype(F32), g_ref[...], b_ref[...]).astype(BF16)
    t_idx = lax.broadcasted_iota(jnp.int32, (CHUNK, CHUNK), 0)
    s_idx = lax.broadcasted_iota(jnp.int32, (CHUNK, CHUNK), 1)
    causal = s_idx <= t_idx
    for g in range(groups):
        cols = slice(g * CHUNK, (g + 1) * CHUNK)
        w = jnp.where(causal, wsp_ref[g], 0.0).astype(BF16)
        bias = bspt_ref[:, g:g + 1]
        for c in range(tg // CHUNK):
            rows = slice(c * CHUNK, (c + 1) * CHUNK)
            mixed = jnp.dot(w, vn[rows, cols], preferred_element_type=F32) + bias
            o_ref[rows, cols] = (u_ref[rows, cols].astype(F32) * mixed).astype(o_ref.dtype)


def _gmlp(proj, ln_g, ln_b, w_sp, b_sp, mix):
    T = proj.shape[0]
    groups = w_sp.shape[0]
    tg = min(512, T)
    row = pl.BlockSpec((1, mix), lambda i: (0, 0))
    return pl.pallas_call(
        functools.partial(_gmlp_kernel, groups=groups),
        out_shape=jax.ShapeDtypeStruct((T, mix), BF16),
        grid=(T // tg,),
        in_specs=[
            pl.BlockSpec((tg, mix), lambda i: (i, 0)),
            pl.BlockSpec((tg, mix), lambda i: (i, 1)),
            row, row,
            pl.BlockSpec((groups, CHUNK, CHUNK), lambda i: (0, 0, 0)),
            pl.BlockSpec((CHUNK, groups), lambda i: (0, 0)),
        ],
        out_specs=pl.BlockSpec((tg, mix), lambda i: (i, 0)),
        compiler_params=_cparams(("parallel",), 10 * _nbytes((tg, mix), F32)),
        name="gmlp",
    )(proj, proj, ln_g.reshape(1, mix), ln_b.reshape(1, mix), w_sp, b_sp.T)


def _attn_kernel(q_ref, kc_ref, kp_ref, vc_ref, vp_ref, o_ref, lse_ref, *, heads, blocks_per_seq):
    has_prev = (pl.program_id(0) % blocks_per_seq) > 0
    qi = lax.broadcasted_iota(jnp.int32, (ATTN_BLOCK, ATTN_BLOCK), 0)
    kj = lax.broadcasted_iota(jnp.int32, (ATTN_BLOCK, ATTN_BLOCK), 1)
    mask_c = kj <= qi
    mask_p = (kj >= qi) & has_prev
    scale = HEAD_DIM ** -0.5
    for h in range(heads):
        cols = slice(h * HEAD_DIM, (h + 1) * HEAD_DIM)
        q = q_ref[:, cols]
        s_c = lax.dot_general(q, kc_ref[:, cols], _NT, preferred_element_type=F32) * scale
        s_p = lax.dot_general(q, kp_ref[:, cols], _NT, preferred_element_type=F32) * scale
        s_c = jnp.where(mask_c, s_c, MASKED)
        s_p = jnp.where(mask_p, s_p, MASKED)
        m = jnp.maximum(jnp.max(s_c, axis=-1, keepdims=True), jnp.max(s_p, axis=-1, keepdims=True))
        p_c = jnp.exp(s_c - m)
        p_p = jnp.exp(s_p - m)
        l = jnp.sum(p_c, axis=-1, keepdims=True) + jnp.sum(p_p, axis=-1, keepdims=True)
        o = jnp.dot(p_c.astype(BF16), vc_ref[:, cols], preferred_element_type=F32)
        o = o + jnp.dot(p_p.astype(BF16), vp_ref[:, cols], preferred_element_type=F32)
        o_ref[:, cols] = (o / l).astype(o_ref.dtype)
        lse_ref[:, cols] = jnp.broadcast_to(m + jnp.log(l), (ATTN_BLOCK, HEAD_DIM))


def _dilated_attention(q2, k2, v2, dilation, seq, mix):
    rows = q2.shape[0]
    nblk = rows // ATTN_BLOCK
    blocks_per_seq = seq // (ATTN_BLOCK * dilation)
    heads = mix // HEAD_DIM
    cur = pl.BlockSpec((ATTN_BLOCK, mix), lambda i, r: (i, r))
    prev = pl.BlockSpec((ATTN_BLOCK, mix), lambda i, r: (jnp.maximum(i - 1, 0), r))
    return pl.pallas_call(
        functools.partial(_attn_kernel, heads=heads, blocks_per_seq=blocks_per_seq),
        out_shape=(jax.ShapeDtypeStruct(q2.shape, BF16), jax.ShapeDtypeStruct(q2.shape, F32)),
        grid=(nblk, dilation),
        in_specs=[cur, cur, prev, cur, prev],
        out_specs=(cur, cur),
        compiler_params=_cparams(("parallel", "parallel"), 24 * _nbytes((ATTN_BLOCK, mix), F32)),
        name=f"dilated_attention_d{dilation}",
    )(q2, k2, k2, v2, v2)


def _combine_kernel(o0, o1, o2, l0, l1, l2, out_ref):
    a, b, c = l0[...], l1[...], l2[...]
    m = jnp.maximum(jnp.maximum(a, b), c)
    ea, eb, ec = jnp.exp(a - m), jnp.exp(b - m), jnp.exp(c - m)
    num = ea * o0[...].astype(F32) + eb * o1[...].astype(F32) + ec * o2[...].astype(F32)
    out_ref[...] = (num / (ea + eb + ec)).astype(out_ref.dtype)


def _combine_groups(outs, lses):
    T, mix = outs[0].shape
    tm = min(512, T)
    blk = pl.BlockSpec((tm, mix), lambda i: (i, 0))
    return pl.pallas_call(
        _combine_kernel,
        out_shape=jax.ShapeDtypeStruct((T, mix), BF16),
        grid=(T // tm,),
        in_specs=[blk] * 6,
        out_specs=blk,
        compiler_params=_cparams(("parallel",), 24 * _nbytes((tm, mix), F32)),
        name="attn_combine",
    )(*outs, *lses)


def _conv_kernel(gb_ref, gc_ref, xin_ref, gcp_ref, xinp_ref, cw_ref, o_ref, *, tiles_per_seq):
    first = (pl.program_id(0) % tiles_per_seq) == 0
    z = gc_ref[...].astype(F32) * xin_ref[...].astype(F32)
    zp = gcp_ref[...].astype(F32) * xinp_ref[...].astype(F32)
    zp = jnp.where(first, 0.0, zp)
    last = zp.shape[0] - 1
    rows = lax.broadcasted_iota(jnp.int32, z.shape, 0)
    z1 = jnp.where(rows == 0, zp[last:last + 1], pltpu.roll(z, 1, 0))
    z2 = jnp.where(rows == 0, zp[last - 1:last], jnp.where(rows == 1, zp[last:last + 1], pltpu.roll(z, 2, 0)))
    y = cw_ref[0:1, :] * z2 + cw_ref[1:2, :] * z1 + cw_ref[2:3, :] * z
    o_ref[...] = (gb_ref[...].astype(F32) * y).astype(o_ref.dtype)


def _short_conv(proj, conv_w, seq, mix, col0):
    T = proj.shape[0]
    tc = min(512, seq)
    tiles_per_seq = seq // tc
    halo = V7X_BF16_SUBLANES
    cb = col0 // mix
    per_halo = tc // halo

    def cur(k):
        return pl.BlockSpec((tc, mix), lambda i: (i, cb + k))

    def prev(k):
        return pl.BlockSpec((halo, mix), lambda i: (jnp.maximum(i * per_halo - 1, 0), cb + k))

    return pl.pallas_call(
        functools.partial(_conv_kernel, tiles_per_seq=tiles_per_seq),
        out_shape=jax.ShapeDtypeStruct((T, mix), BF16),
        grid=(T // tc,),
        in_specs=[cur(0), cur(1), cur(2), prev(1), prev(2),
                  pl.BlockSpec(conv_w.shape, lambda i: (0, 0))],
        out_specs=pl.BlockSpec((tc, mix), lambda i: (i, 0)),
        compiler_params=_cparams(("parallel",), 16 * _nbytes((tc, mix), F32)),
        name="short_conv",
    )(proj, proj, proj, proj, proj, conv_w)


def _merge_kernel(h_ref, ba_ref, bb_ref, bc_ref, wg_ref, bg_ref, wb_ref, o_ref):
    h = h_ref[...]
    total = None
    for g, br in enumerate((ba_ref, bb_ref, bc_ref)):
        gate = jax.nn.sigmoid(jnp.dot(h, wg_ref[g], preferred_element_type=F32) + bg_ref[g])
        term = gate * jnp.dot(br[...], wb_ref[g], preferred_element_type=F32)
        total = term if total is None else total + term
    o_ref[...] = total.astype(o_ref.dtype)


def _merge(h, branches, w_gate, b_gate, w_branch):
    T, D = h.shape
    mix = branches[0].shape[1]
    tm = min(1024, T)
    tn = min(256, D)
    br = pl.BlockSpec((tm, mix), lambda i, j: (i, 0), pipeline_mode=pl.Buffered(1))
    need = 2 * (_nbytes((tm, D), BF16) + 3 * _nbytes((D, tn), BF16) + 3 * _nbytes((mix, tn), BF16)
                + _nbytes((tm, tn), BF16)) + 3 * _nbytes((tm, mix), BF16) + 10 * _nbytes((tm, tn), F32)
    return pl.pallas_call(
        _merge_kernel,
        out_shape=jax.ShapeDtypeStruct((T, D), BF16),
        grid=(T // tm, D // tn),
        in_specs=[
            pl.BlockSpec((tm, D), lambda i, j: (i, 0)),
            br, br, br,
            pl.BlockSpec((N_BRANCH, D, tn), lambda i, j: (0, 0, j)),
            pl.BlockSpec((N_BRANCH, 1, tn), lambda i, j: (0, 0, j)),
            pl.BlockSpec((N_BRANCH, mix, tn), lambda i, j: (0, 0, j)),
        ],
        out_specs=pl.BlockSpec((tm, tn), lambda i, j: (i, j)),
        compiler_params=_cparams(("parallel", "arbitrary"), need),
        name="branch_merge",
    )(h, *branches, w_gate, b_gate.reshape(N_BRANCH, 1, D), w_branch)


def _matmul_kernel(a_ref, w_ref, o_ref):
    o_ref[...] = jnp.dot(a_ref[...], w_ref[...], preferred_element_type=F32).astype(o_ref.dtype)


def _matmul(a, w, out_dtype, name):
    T, K = a.shape
    N = w.shape[1]
    tm = min(1024, T)
    tn = min(512, N)
    need = 2 * (_nbytes((tm, K), BF16) + _nbytes((K, tn), BF16) + _nbytes((tm, tn), out_dtype)) \
        + 2 * _nbytes((tm, tn), F32)
    return pl.pallas_call(
        _matmul_kernel,
        out_shape=jax.ShapeDtypeStruct((T, N), out_dtype),
        grid=(T // tm, N // tn),
        in_specs=[pl.BlockSpec((tm, K), lambda i, j: (i, 0)), pl.BlockSpec((K, tn), lambda i, j: (0, j))],
        out_specs=pl.BlockSpec((tm, tn), lambda i, j: (i, j)),
        compiler_params=_cparams(("parallel", "arbitrary"), need),
        name=name,
    )(a, w)


def _residual_ln_kernel(x_ref, y_ref, gate_ref, lg_ref, lb_ref, *rest, alpha, y_transposed, emit_h):
    y = y_ref[...].astype(F32)
    if y_transposed:
        y = y.T
    xn = _layer_norm_rows(alpha * x_ref[...] + gate_ref[...] * y, lg_ref[...], lb_ref[...])
    if emit_h:
        sc_ref, sh_ref, xo_ref, ho_ref = rest
        ho_ref[...] = (xn * (1.0 + sc_ref[...]) + sh_ref[...]).astype(ho_ref.dtype)
    else:
        (xo_ref,) = rest
    xo_ref[...] = xn


def _residual_ln(x, y, modr, gate_row, ln_g, ln_b, alpha, *, y_transposed, next_rows=None):
    B, S, D = x.shape
    ts = min(256, S)
    nst = S // ts
    emit_h = next_rows is not None

    def mod_spec(row_fn):
        return pl.BlockSpec((None, 1, D), lambda b, s: (row_fn(b), 0, 0))

    vec = pl.BlockSpec((1, D), lambda b, s: (0, 0))
    if y_transposed:
        y_spec = pl.BlockSpec((D, ts), lambda b, s: (0, b * nst + s))
    else:
        y_spec = pl.BlockSpec((ts, D), lambda b, s: (b * nst + s, 0))
    x_spec = pl.BlockSpec((None, ts, D), lambda b, s: (b, s, 0))
    in_specs = [x_spec, y_spec, mod_spec(gate_row), vec, vec]
    args = [x, y, modr, ln_g.reshape(1, D), ln_b.reshape(1, D)]
    out_shape = [jax.ShapeDtypeStruct((B, S, D), F32)]
    out_specs = [x_spec]
    if emit_h:
        in_specs += [mod_spec(next_rows[0]), mod_spec(next_rows[1])]
        args += [modr, modr]
        out_shape.append(jax.ShapeDtypeStruct((B * S, D), BF16))
        out_specs.append(pl.BlockSpec((ts, D), lambda b, s: (b * nst + s, 0)))
    res = pl.pallas_call(
        functools.partial(_residual_ln_kernel, alpha=alpha, y_transposed=y_transposed, emit_h=emit_h),
        out_shape=tuple(out_shape),
        grid=(B, nst),
        in_specs=in_specs,
        out_specs=tuple(out_specs),
        compiler_params=_cparams(("parallel", "parallel"), 12 * _nbytes((ts, D), F32)),
        name="residual_ln",
    )(*args)
    return res if emit_h else (res[0], None)


def _peer_scores_kernel(h_ref, wpq_ref, keys_ref, st_ref):
    q = jnp.dot(h_ref[...], wpq_ref[...], preferred_element_type=F32).astype(BF16)
    st_ref[...] = lax.dot_general(keys_ref[...], q, _NT, preferred_element_type=F32)


def _peer_scores(h, w_pq, keys_bd):
    T, D = h.shape
    HQ = w_pq.shape[1]
    R = keys_bd.shape[0]
    tm = min(512, T)
    need = 2 * (_nbytes((tm, D), BF16) + _nbytes((D, HQ), BF16) + _nbytes((R, HQ), BF16)
                + _nbytes((R, tm), F32)) + 2 * _nbytes((tm, HQ), F32)
    return pl.pallas_call(
        _peer_scores_kernel,
        out_shape=jax.ShapeDtypeStruct((R, T), F32),
        grid=(T // tm,),
        in_specs=[
            pl.BlockSpec((tm, D), lambda i: (i, 0)),
            pl.BlockSpec((D, HQ), lambda i: (0, 0)),
            pl.BlockSpec((R, HQ), lambda i: (0, 0)),
        ],
        out_specs=pl.BlockSpec((R, tm), lambda i: (0, i)),
        compiler_params=_cparams(("parallel",), need),
        name="peer_scores",
    )(h, w_pq, keys_bd)


def _top16_ranks(s, row_ids):
    n = s.shape[0]
    k_ids = lax.broadcasted_iota(jnp.int32, (PEER_TOPK, s.shape[1]), 0)
    rank = jnp.full(s.shape, float(PEER_TOPK), F32)
    tops = jnp.zeros((PEER_TOPK, s.shape[1]), F32)
    for k in range(PEER_TOPK):
        m = jnp.max(s, axis=0, keepdims=True)
        first = jnp.min(jnp.where(s == m, row_ids, float(n)), axis=0, keepdims=True)
        sel = row_ids == first
        rank = jnp.where(sel, float(k), rank)
        s = jnp.where(sel, -jnp.inf, s)
        tops = jnp.where(k_ids == k, m, tops)
    return rank, tops


def _pair_counts(t1, t2):
    L = t1.shape[1]
    K = PEER_TOPK
    sub = 8
    pieces = [t1[0:1] + t2]
    pos = [lax.broadcasted_iota(jnp.int32, (K, L), 0).astype(F32)]
    r8 = lax.broadcasted_iota(jnp.int32, (sub, L), 0).astype(F32)
    for a in range(1, sub):
        pieces.append(t1[a:a + 1] + t2[0:sub])
        pos.append(r8 + float(a * K))
    pieces.append(t1[sub:K] + t2[0:1])
    pos.append((r8 + float(sub)) * float(K))
    v = jnp.concatenate(pieces, axis=0)
    p = jnp.concatenate(pos, axis=0)
    picked = jnp.zeros(v.shape, F32)
    z = jnp.zeros((1, L), F32)
    top = None
    for k in range(K):
        m = jnp.max(v, axis=0, keepdims=True)
        first = jnp.min(jnp.where(v == m, p, float(K * K)), axis=0, keepdims=True)
        sel = p == first
        picked = jnp.where(sel, 1.0, picked)
        v = jnp.where(sel, -jnp.inf, v)
        if k == 0:
            top = m
            z = z + 1.0
        else:
            z = z + jnp.exp(m - top)
    a_ids = lax.broadcasted_iota(jnp.int32, (sub, L), 0)
    low = jnp.zeros((sub, L), F32)
    low = jnp.where(a_ids == 0, jnp.sum(picked[0:K], axis=0, keepdims=True), low)
    for a in range(1, sub):
        off = K + (a - 1) * sub
        low = jnp.where(a_ids == a, jnp.sum(picked[off:off + sub], axis=0, keepdims=True), low)
    counts = jnp.concatenate([low, picked[K + (sub - 1) * sub:]], axis=0)
    return counts, z


def _peer_select_kernel(st_ref, rank2_ref, cnt1_ref, e1_ref, e2_ref, *, heads, nk):
    tt = st_ref.shape[1]
    row_ids = lax.broadcasted_iota(jnp.int32, (nk, V7X_LANES), 0).astype(F32)

    def per_head(idx, carry):
        c = idx // heads
        hh = idx % heads
        lanes = pl.ds(pl.multiple_of(c * V7X_LANES, V7X_LANES), V7X_LANES)
        r1 = pl.ds(pl.multiple_of((2 * hh) * nk, nk), nk)
        r2 = pl.ds(pl.multiple_of((2 * hh + 1) * nk, nk), nk)
        ro = pl.ds(pl.multiple_of(hh * nk, nk), nk)
        s1 = st_ref[r1, lanes]
        s2 = st_ref[r2, lanes]
        rank1, t1 = _top16_ranks(s1, row_ids)
        rank2, t2 = _top16_ranks(s2, row_ids)
        counts, z = _pair_counts(t1, t2)
        cnt1 = jnp.zeros((nk, V7X_LANES), F32)
        for a in range(PEER_TOPK):
            cnt1 = jnp.where(rank1 == float(a), counts[a:a + 1], cnt1)
        rank2_ref[ro, lanes] = rank2
        cnt1_ref[ro, lanes] = cnt1
        e1_ref[ro, lanes] = jnp.exp(s1 - t1[0:1]) / z
        e2_ref[ro, lanes] = jnp.exp(s2 - t2[0:1])
        return carry

    lax.fori_loop(0, (tt // V7X_LANES) * heads, per_head, 0)


def _peer_select(st, heads, nk):
    R, T = st.shape
    tt = min(512, T)
    out = jax.ShapeDtypeStruct((heads * nk, T), F32)
    blk = pl.BlockSpec((heads * nk, tt), lambda i: (0, i))
    return pl.pallas_call(
        functools.partial(_peer_select_kernel, heads=heads, nk=nk),
        out_shape=(out,) * 4,
        grid=(T // tt,),
        in_specs=[pl.BlockSpec((R, tt), lambda i: (0, i))],
        out_specs=(blk,) * 4,
        compiler_params=_cparams(("parallel",), 2 * _nbytes((R, tt), F32) + 8 * _nbytes((heads * nk, tt), F32)),
        name="peer_select",
    )(st)


def _peer_dense_kernel(h_ref, wu_ref, wvt_ref, rank2_ref, cnt1_ref, e1_ref, e2_ref, yt_ref, a_ref,
                       *, heads, nk, keys_per_step):
    e = pl.program_id(1)

    @pl.when(e == 0)
    def _():
        yt_ref[...] = jnp.zeros_like(yt_ref)

    act = lax.dot_general(wu_ref[...], h_ref[...], _NT, preferred_element_type=F32)
    for b in range(keys_per_step):
        i1 = e * keys_per_step + b
        gate = None
        for hh in range(heads):
            row = pl.ds(hh * nk + i1, 1)
            blk = slice(hh * nk, (hh + 1) * nk)
            term = jnp.where(rank2_ref[blk, :] < cnt1_ref[row, :], e2_ref[blk, :] * e1_ref[row, :], 0.0)
            gate = term if gate is None else gate + term
        rows = slice(b * nk, (b + 1) * nk)
        a_ref[rows, :] = (jax.nn.gelu(act[rows, :]) * gate).astype(a_ref.dtype)
    yt_ref[...] += jnp.dot(wvt_ref[...], a_ref[...], preferred_element_type=F32)


def _peer_dense(h, w_u, w_vt, sel, heads, nk):
    T, D = h.shape
    E = w_u.shape[0]
    tm = min(512, T)
    keys_per_step = min(4, nk)
    te = keys_per_step * nk
    sel_spec = pl.BlockSpec((heads * nk, tm), lambda i, e: (0, i), pipeline_mode=pl.Buffered(1))
    need = _nbytes((tm, D), BF16) + 4 * _nbytes((te, D), BF16) + 2 * _nbytes((D, tm), F32) \
        + 4 * _nbytes((heads * nk, tm), F32) + _nbytes((te, tm), BF16) + 4 * _nbytes((te, tm), F32)
    return pl.pallas_call(
        functools.partial(_peer_dense_kernel, heads=heads, nk=nk, keys_per_step=keys_per_step),
        out_shape=jax.ShapeDtypeStruct((D, T), F32),
        grid=(T // tm, E // te),
        in_specs=[
            pl.BlockSpec((tm, D), lambda i, e: (i, 0), pipeline_mode=pl.Buffered(1)),
            pl.BlockSpec((te, D), lambda i, e: (e, 0)),
            pl.BlockSpec((D, te), lambda i, e: (0, e)),
            sel_spec, sel_spec, sel_spec, sel_spec,
        ],
        out_specs=pl.BlockSpec((D, tm), lambda i, e: (0, i)),
        scratch_shapes=[pltpu.VMEM((te, tm), BF16)],
        compiler_params=_cparams(("parallel", "arbitrary"), need),
        name="peer_dense",
    )(h, w_u, w_vt, *sel)


def _class_view(a, dilation):
    T, mix = a.shape
    return a.reshape(T // dilation, dilation * mix)


def _mixer_sublayer(h, proj_w, cos_t, sin_t, w_gate, b_gate, ln_v_g, ln_v_b, w_sp, b_sp, conv_w,
                    w_branch, w_o, seq):
    mix = ln_v_g.shape[0]
    T = h.shape[0]
    proj = _proj_in(h, proj_w, cos_t, sin_t, mix)
    branch_a = _gmlp(proj, ln_v_g, ln_v_b, w_sp, b_sp, mix)
    q0 = 2 * mix
    outs, lses = [], []
    for g, (_, dilation) in enumerate(B_PATTERNS):
        q = proj[:, q0 + g * mix: q0 + (g + 1) * mix]
        k = proj[:, q0 + (3 + g) * mix: q0 + (4 + g) * mix]
        v = proj[:, q0 + (6 + g) * mix: q0 + (7 + g) * mix]
        o, lse = _dilated_attention(_class_view(q, dilation), _class_view(k, dilation),
                                    _class_view(v, dilation), dilation, seq, mix)
        outs.append(o.reshape(T, mix))
        lses.append(lse.reshape(T, mix))
    branch_b = _combine_groups(outs, lses)
    branch_c = _short_conv(proj, conv_w, seq, mix, q0 + 9 * mix)
    merged = _merge(h, (branch_a, branch_b, branch_c), w_gate, b_gate, w_branch)
    return _matmul(merged, w_o, BF16, "out_proj")


def _peer_sublayer(h, w_pq, keys_bd, w_u, w_vt, heads, nk):
    st = _peer_scores(h, w_pq, keys_bd)
    sel = _peer_select(st, heads, nk)
    return _peer_dense(h, w_u, w_vt, sel, heads, nk)


def kernel(x, c, positions, w_ada, b_ada, w_in, w_gate, b_gate, ln_v_g, ln_v_b, w_sp, b_sp,
           conv_w, w_branch, w_o, ln1_g, ln1_b, w_pq, sub_keys, w_u, w_v, ln2_g, ln2_b):
    B, S, D = x.shape
    depth = w_ada.shape[0]
    heads, _, nk, _ = sub_keys.shape[1:]
    alpha = (2 * depth) ** 0.25
    assert S % (ATTN_BLOCK * B_PATTERNS[-1][1]) == 0

    cos_t, sin_t = _rope_tables(positions)
    modr = _adaln(c, w_ada, b_ada)

    def mod_row(layer, k):
        return lambda b: (layer * B + b) * 6 + k

    h = _modulate(x, modr, mod_row(0, 1), mod_row(0, 0))
    for l in range(depth):
        y = _mixer_sublayer(h, w_in[l].astype(BF16), cos_t, sin_t, w_gate[l].astype(BF16), b_gate[l],
                            ln_v_g[l], ln_v_b[l], w_sp[l], b_sp[l], conv_w[l],
                            w_branch[l].astype(BF16), w_o[l].astype(BF16), S)
        x, h = _residual_ln(x, y, modr, mod_row(l, 2), ln1_g[l], ln1_b[l], alpha, y_transposed=False,
                            next_rows=(mod_row(l, 4), mod_row(l, 3)))
        keys_bd = jax.scipy.linalg.block_diag(
            *[sub_keys[l, hh, p] for hh in range(heads) for p in range(2)]).astype(BF16)
        y_t = _peer_sublayer(h, w_pq[l].astype(BF16), keys_bd, w_u[l].astype(BF16),
                             w_v[l].T.astype(BF16), heads, nk)
        nxt = (mod_row(l + 1, 1), mod_row(l + 1, 0)) if l + 1 < depth else None
        x, h = _residual_ln(x, y_t, modr, mod_row(l, 5), ln2_g[l], ln2_b[l], alpha, y_transposed=True,
                            next_rows=nxt)
    return x
```

```python
import functools

import jax
import jax.numpy as jnp
from jax import lax
from jax.experimental import pallas as pl
from jax.experimental.pallas import tpu as pltpu

F32 = jnp.float32
BF16 = jnp.bfloat16

HEAD_DIM = 128
CHUNK = 128
ATTN_BLOCK = 128
ROT_DIM = HEAD_DIM // 4
ROPE_THETA = 500000.0
B_PATTERNS = ((128, 1), (512, 4), (2048, 16))
N_BRANCH = 3
PEER_TOPK = 16
LN_EPS = 1e-5
MASKED = -1e30

V7X_LANES = 128
V7X_BF16_SUBLANES = 16
V7X_VMEM_LIMIT_CAP = 58 * 2**20
SPILL_AND_TEMP_BYTES = 4 * 2**20
ATTN_VMEM_BUDGET = 32 * 2**20


def _cparams(semantics, vmem_bytes, flags=None):
    limit = max(vmem_bytes + SPILL_AND_TEMP_BYTES, 16 * 2**20)
    return pltpu.CompilerParams(
        dimension_semantics=semantics,
        vmem_limit_bytes=int(min(limit, V7X_VMEM_LIMIT_CAP)),
        flags=flags,
    )


def _nbytes(shape, dtype):
    n = 1
    for s in shape:
        n *= s
    return n * jnp.dtype(dtype).itemsize


def _layer_norm_rows(z, g, b):
    mu = jnp.mean(z, axis=-1, keepdims=True)
    zc = z - mu
    var = jnp.mean(zc * zc, axis=-1, keepdims=True)
    return zc * lax.rsqrt(var + LN_EPS) * g + b


_NT = (((1,), (1,)), ((), ()))


def _rope_table_kernel(pos_ref, freq_ref, sign_ref, cos_ref, sin_ref):
    ang = pos_ref[...].astype(F32) * freq_ref[...]
    cos_ref[...] = jnp.cos(ang)
    sin_ref[...] = jnp.sin(ang) * sign_ref[...]


def _rope_tables(positions):
    T = positions.size
    half = ROT_DIM // 2
    inv_freq = ROPE_THETA ** (-jnp.arange(half, dtype=F32) / half)
    zeros = jnp.zeros((HEAD_DIM - ROT_DIM,), F32)
    freq = jnp.concatenate([inv_freq, inv_freq, zeros])[None, :]
    sign = jnp.concatenate([-jnp.ones((half,), F32), jnp.ones((half,), F32), zeros])[None, :]
    tm = min(1024, T)
    row = pl.BlockSpec((1, HEAD_DIM), lambda i: (0, 0))
    tab = pl.BlockSpec((tm, HEAD_DIM), lambda i: (i, 0))
    return pl.pallas_call(
        _rope_table_kernel,
        out_shape=(jax.ShapeDtypeStruct((T, HEAD_DIM), F32),) * 2,
        grid=(T // tm,),
        in_specs=[pl.BlockSpec((tm, 1), lambda i: (i, 0)), row, row],
        out_specs=(tab, tab),
        compiler_params=_cparams(("parallel",), 8 * _nbytes((tm, HEAD_DIM), F32)),
        name="rope_table",
    )(positions.reshape(T, 1), freq, sign)


def _adaln_kernel(c_ref, w_ref, b_ref, o_ref):
    c = c_ref[...]
    o_ref[...] = jnp.dot(c * jax.nn.sigmoid(c), w_ref[...], preferred_element_type=F32) + b_ref[...]


def _adaln(c, w_ada, b_ada):
    L, D, N = w_ada.shape
    B = c.shape[0]
    rows = 8
    c_pad = jnp.zeros((rows, D), F32).at[:B].set(c)
    tn = min(512, N)
    mod = pl.pallas_call(
        _adaln_kernel,
        out_shape=jax.ShapeDtypeStruct((L, rows, N), F32),
        grid=(L, N // tn),
        in_specs=[
            pl.BlockSpec((rows, D), lambda l, j: (0, 0)),
            pl.BlockSpec((None, D, tn), lambda l, j: (l, 0, j)),
            pl.BlockSpec((None, 1, tn), lambda l, j: (l, 0, j)),
        ],
        out_specs=pl.BlockSpec((None, rows, tn), lambda l, j: (l, 0, j)),
        compiler_params=_cparams(("parallel", "parallel"), 2 * _nbytes((D, tn), F32) + 2**22),
        name="adaln",
    )(c_pad, w_ada, b_ada.reshape(L, 1, N))
    return mod[:, :B].reshape(L * B * 6, 1, D)


def _modulate_kernel(x_ref, sc_ref, sh_ref, o_ref):
    o_ref[...] = (x_ref[...] * (1.0 + sc_ref[...]) + sh_ref[...]).astype(o_ref.dtype)


def _modulate(x, modr, sc_row, sh_row):
    B, S, D = x.shape
    ts = min(512, S)
    nst = S // ts
    return pl.pallas_call(
        _modulate_kernel,
        out_shape=jax.ShapeDtypeStruct((B * S, D), BF16),
        grid=(B, nst),
        in_specs=[
            pl.BlockSpec((None, ts, D), lambda b, s: (b, s, 0)),
            pl.BlockSpec((None, 1, D), lambda b, s: (sc_row(b), 0, 0)),
            pl.BlockSpec((None, 1, D), lambda b, s: (sh_row(b), 0, 0)),
        ],
        out_specs=pl.BlockSpec((ts, D), lambda b, s: (b * nst + s, 0)),
        compiler_params=_cparams(("parallel", "parallel"), 3 * _nbytes((ts, D), F32) + 2**22),
        name="modulate",
    )(x, modr, modr)


M_SPLIT = 2


def _row_chunks(tm):
    rows = tm // M_SPLIT
    return [slice(c * rows, (c + 1) * rows) for c in range(M_SPLIT)]


def _proj_ac_kernel(a_ref, w_ref, o_ref, *, n_gelu):
    j = pl.program_id(1)

    @pl.when(j < n_gelu)
    def _():
        for rs in _row_chunks(a_ref.shape[0]):
            acc = jnp.dot(a_ref[rs, :], w_ref[...], preferred_element_type=F32)
            o_ref[rs, :] = jax.nn.gelu(acc).astype(o_ref.dtype)

    @pl.when(j >= n_gelu)
    def _():
        for rs in _row_chunks(a_ref.shape[0]):
            acc = jnp.dot(a_ref[rs, :], w_ref[...], preferred_element_type=F32)
            o_ref[rs, :] = acc.astype(o_ref.dtype)


def _proj_ac(h, w_ac, mix):
    T, D = h.shape
    N = w_ac.shape[1]
    tm = min(1024, T)
    tn = min(512, mix)
    need = 2 * (_nbytes((tm, D), BF16) + _nbytes((D, tn), BF16) + _nbytes((tm, tn), BF16)) \
        + 4 * _nbytes((tm, tn), F32)
    return pl.pallas_call(
        functools.partial(_proj_ac_kernel, n_gelu=2 * mix // tn),
        out_shape=jax.ShapeDtypeStruct((T, N), BF16),
        grid=(T // tm, N // tn),
        in_specs=[pl.BlockSpec((tm, D), lambda i, j: (i, 0)), pl.BlockSpec((D, tn), lambda i, j: (0, j))],
        out_specs=pl.BlockSpec((tm, tn), lambda i, j: (i, j)),
        compiler_params=_cparams(("parallel", "arbitrary"), need),
        name="proj_ac",
    )(h, w_ac)


def _store_heads(o_ref, rs, val):
    for hh in range(val.shape[1] // HEAD_DIM):
        o_ref[hh, rs, :] = val[:, hh * HEAD_DIM:(hh + 1) * HEAD_DIM]


def _proj_qkv_kernel(a_ref, w_ref, cos_ref, sin_ref, o_ref, *, n_rope):
    j = pl.program_id(1)
    tn = w_ref.shape[1]

    @pl.when(j < n_rope)
    def _():
        reps = tn // HEAD_DIM
        half = ROT_DIM // 2
        for rs in _row_chunks(a_ref.shape[0]):
            acc = jnp.dot(a_ref[rs, :], w_ref[...], preferred_element_type=F32)
            cos = jnp.concatenate([cos_ref[rs, :]] * reps, axis=1)
            sin = jnp.concatenate([sin_ref[rs, :]] * reps, axis=1)
            lane = lax.broadcasted_iota(jnp.int32, acc.shape, 1) % HEAD_DIM
            partner = jnp.where(lane < half, pltpu.roll(acc, tn - half, 1), pltpu.roll(acc, half, 1))
            _store_heads(o_ref, rs, acc * cos + partner * sin)

    @pl.when(j >= n_rope)
    def _():
        for rs in _row_chunks(a_ref.shape[0]):
            _store_heads(o_ref, rs, jnp.dot(a_ref[rs, :], w_ref[...], preferred_element_type=F32))


def _proj_qkv(h, w_qkv, cos_t, sin_t, mix):
    T, D = h.shape
    N = w_qkv.shape[1]
    tm = min(1024, T)
    tn = min(512, mix)
    hpt = tn // HEAD_DIM
    tab = pl.BlockSpec((tm, HEAD_DIM), lambda i, j: (i, 0))
    need = 2 * (_nbytes((tm, D), BF16) + _nbytes((D, tn), BF16) + _nbytes((tm, tn), F32)) \
        + 4 * _nbytes((tm, HEAD_DIM), F32) + 6 * _nbytes((tm, tn), F32)
    return pl.pallas_call(
        functools.partial(_proj_qkv_kernel, n_rope=6 * mix // tn),
        out_shape=jax.ShapeDtypeStruct((N // HEAD_DIM, T, HEAD_DIM), F32),
        grid=(T // tm, N // tn),
        in_specs=[
            pl.BlockSpec((tm, D), lambda i, j: (i, 0)),
            pl.BlockSpec((D, tn), lambda i, j: (0, j)),
            tab, tab,
        ],
        out_specs=pl.BlockSpec((hpt, tm, HEAD_DIM), lambda i, j: (j, i, 0)),
        compiler_params=_cparams(("parallel", "arbitrary"), need),
        name="proj_qkv",
    )(h, w_qkv, cos_t, sin_t)


def _gmlp_kernel(u_ref, v_ref, g_ref, b_ref, wsp_ref, bspt_ref, o_ref, *, groups):
    tg = u_ref.shape[0]
    vn = _layer_norm_rows(v_ref[...].astype(F32), g_ref[...], b_ref[...]).astype(BF16)
    t_idx = lax.broadcasted_iota(jnp.int32, (CHUNK, CHUNK), 0)
    s_idx = lax.broadcasted_iota(jnp.int32, (CHUNK, CHUNK), 1)
    causal = s_idx <= t_idx
    for g in range(groups):
        cols = slice(g * CHUNK, (g + 1) * CHUNK)
        w = jnp.where(causal, wsp_ref[g], 0.0).astype(BF16)
        bias = bspt_ref[:, g:g + 1]
        for c in range(tg // CHUNK):
            rows = slice(c * CHUNK, (c + 1) * CHUNK)
            mixed = jnp.dot(w, vn[rows, cols], preferred_element_type=F32) + bias
            o_ref[rows, cols] = (u_ref[rows, cols].astype(F32) * mixed).astype(o_ref.dtype)


def _gmlp(proj, ln_g, ln_b, w_sp, b_sp, mix):
    T = proj.shape[0]
    groups = w_sp.shape[0]
    tg = min(512, T)
    row = pl.BlockSpec((1, mix), lambda i: (0, 0))
    return pl.pallas_call(
        functools.partial(_gmlp_kernel, groups=groups),
        out_shape=jax.ShapeDtypeStruct((T, mix), BF16),
        grid=(T // tg,),
        in_specs=[
            pl.BlockSpec((tg, mix), lambda i: (i, 0)),
            pl.BlockSpec((tg, mix), lambda i: (i, 1)),
            row, row,
            pl.BlockSpec((groups, CHUNK, CHUNK), lambda i: (0, 0, 0)),
            pl.BlockSpec((CHUNK, groups), lambda i: (0, 0)),
        ],
        out_specs=pl.BlockSpec((tg, mix), lambda i: (i, 0)),
        compiler_params=_cparams(("parallel",), 10 * _nbytes((tg, mix), F32)),
        name="gmlp",
    )(proj, proj, ln_g.reshape(1, mix), ln_b.reshape(1, mix), w_sp, b_sp.T)


def _attn_kernel(q_ref, kc_ref, kp_ref, vc_ref, vp_ref, o_ref, lse_ref, *, dilation, periods_per_seq):
    has_prev = (pl.program_id(0) % periods_per_seq) > 0
    qi = lax.broadcasted_iota(jnp.int32, (ATTN_BLOCK, ATTN_BLOCK), 0)
    kj = lax.broadcasted_iota(jnp.int32, (ATTN_BLOCK, ATTN_BLOCK), 1)
    mask_c = kj <= qi
    mask_p = (kj >= qi) & has_prev
    scale = HEAD_DIM ** -0.5

    def one_class(r, carry):
        rows = pl.ds(r, ATTN_BLOCK, stride=dilation) if dilation > 1 else pl.ds(0, ATTN_BLOCK)
        for h in range(q_ref.shape[0]):
            q = q_ref[h, rows, :].astype(BF16)
            s_c = lax.dot_general(q, kc_ref[h, rows, :].astype(BF16), _NT, preferred_element_type=F32) * scale
            s_p = lax.dot_general(q, kp_ref[h, rows, :].astype(BF16), _NT, preferred_element_type=F32) * scale
            s_c = jnp.where(mask_c, s_c, MASKED)
            s_p = jnp.where(mask_p, s_p, MASKED)
            m = jnp.maximum(jnp.max(s_c, axis=-1, keepdims=True), jnp.max(s_p, axis=-1, keepdims=True))
            p_c = jnp.exp(s_c - m)
            p_p = jnp.exp(s_p - m)
            l = jnp.sum(p_c, axis=-1, keepdims=True) + jnp.sum(p_p, axis=-1, keepdims=True)
            o = jnp.dot(p_c.astype(BF16), vc_ref[h, rows, :].astype(BF16), preferred_element_type=F32)
            o = o + jnp.dot(p_p.astype(BF16), vp_ref[h, rows, :].astype(BF16), preferred_element_type=F32)
            o_ref[h, rows, :] = o / l
            lse_ref[h, rows, :] = jnp.broadcast_to(m + jnp.log(l), (ATTN_BLOCK, HEAD_DIM))
        return carry

    if dilation > 1:
        lax.fori_loop(0, dilation, one_class, 0)
    else:
        one_class(0, 0)


def _dilated_attention(qkv, group, dilation, seq, heads):
    T = qkv.shape[1]
    period = ATTN_BLOCK * dilation
    periods_per_seq = seq // period
    hps = heads
    while 14 * _nbytes((hps, period, HEAD_DIM), F32) > ATTN_VMEM_BUDGET and hps % 2 == 0:
        hps //= 2

    def cur(part):
        return pl.BlockSpec((hps, period, HEAD_DIM), lambda p, h: ((3 * part + group) * (heads // hps) + h, p, 0))

    def prev(part):
        return pl.BlockSpec((hps, period, HEAD_DIM),
                            lambda p, h: ((3 * part + group) * (heads // hps) + h, jnp.maximum(p - 1, 0), 0))

    out = pl.BlockSpec((hps, period, HEAD_DIM), lambda p, h: (h, p, 0))
    shape = jax.ShapeDtypeStruct((heads, T, HEAD_DIM), F32)
    return pl.pallas_call(
        functools.partial(_attn_kernel, dilation=dilation, periods_per_seq=periods_per_seq),
        out_shape=(shape, shape),
        grid=(T // period, heads // hps),
        in_specs=[cur(0), cur(1), prev(1), cur(2), prev(2)],
        out_specs=(out, out),
        compiler_params=_cparams(("parallel", "parallel"), 14 * _nbytes((hps, period, HEAD_DIM), F32)),
        name=f"dilated_attention_d{dilation}",
    )(qkv, qkv, qkv, qkv, qkv)


def _combine_kernel(o0, o1, o2, l0, l1, l2, out_ref):
    for h in range(o0.shape[0]):
        a, b, c = l0[h], l1[h], l2[h]
        m = jnp.maximum(jnp.maximum(a, b), c)
        ea, eb, ec = jnp.exp(a - m), jnp.exp(b - m), jnp.exp(c - m)
        num = ea * o0[h] + eb * o1[h] + ec * o2[h]
        out_ref[:, h * HEAD_DIM:(h + 1) * HEAD_DIM] = (num / (ea + eb + ec)).astype(out_ref.dtype)


def _combine_groups(outs, lses):
    heads, T, _ = outs[0].shape
    tm = min(512, T)
    blk = pl.BlockSpec((heads, tm, HEAD_DIM), lambda i: (0, i, 0))
    return pl.pallas_call(
        _combine_kernel,
        out_shape=jax.ShapeDtypeStruct((T, heads * HEAD_DIM), BF16),
        grid=(T // tm,),
        in_specs=[blk] * 6,
        out_specs=pl.BlockSpec((tm, heads * HEAD_DIM), lambda i: (i, 0)),
        compiler_params=_cparams(("parallel",), 14 * _nbytes((heads, tm, HEAD_DIM), F32)),
        name="attn_combine",
    )(*outs, *lses)


def _conv_kernel(gb_ref, gc_ref, xin_ref, gcp_ref, xinp_ref, cw_ref, o_ref, *, tiles_per_seq):
    first = (pl.program_id(0) % tiles_per_seq) == 0
    z = gc_ref[...].astype(F32) * xin_ref[...].astype(F32)
    zp = gcp_ref[...].astype(F32) * xinp_ref[...].astype(F32)
    zp = jnp.where(first, 0.0, zp)
    last = zp.shape[0] - 1
    rows = lax.broadcasted_iota(jnp.int32, z.shape, 0)
    z1 = jnp.where(rows == 0, zp[last:last + 1], pltpu.roll(z, 1, 0))
    z2 = jnp.where(rows == 0, zp[last - 1:last], jnp.where(rows == 1, zp[last:last + 1], pltpu.roll(z, 2, 0)))
    y = cw_ref[0:1, :] * z2 + cw_ref[1:2, :] * z1 + cw_ref[2:3, :] * z
    o_ref[...] = (gb_ref[...].astype(F32) * y).astype(o_ref.dtype)


def _short_conv(proj, conv_w, seq, mix, col0):
    T = proj.shape[0]
    tc = min(512, seq)
    tiles_per_seq = seq // tc
    halo = V7X_BF16_SUBLANES
    cb = col0 // mix
    per_halo = tc // halo

    def cur(k):
        return pl.BlockSpec((tc, mix), lambda i: (i, cb + k))

    def prev(k):
        return pl.BlockSpec((halo, mix), lambda i: (jnp.maximum(i * per_halo - 1, 0), cb + k))

    return pl.pallas_call(
        functools.partial(_conv_kernel, tiles_per_seq=tiles_per_seq),
        out_shape=jax.ShapeDtypeStruct((T, mix), BF16),
        grid=(T // tc,),
        in_specs=[cur(0), cur(1), cur(2), prev(1), prev(2),
                  pl.BlockSpec(conv_w.shape, lambda i: (0, 0))],
        out_specs=pl.BlockSpec((tc, mix), lambda i: (i, 0)),
        compiler_params=_cparams(("parallel",), 16 * _nbytes((tc, mix), F32)),
        name="short_conv",
    )(proj, proj, proj, proj, proj, conv_w)


def _merge_kernel(h_ref, ba_ref, bb_ref, bc_ref, wg_ref, bg_ref, wb_ref, o_ref):
    h = h_ref[...]
    total = None
    for g, br in enumerate((ba_ref, bb_ref, bc_ref)):
        gate = jax.nn.sigmoid(jnp.dot(h, wg_ref[g], preferred_element_type=F32) + bg_ref[g])
        term = gate * jnp.dot(br[...], wb_ref[g], preferred_element_type=F32)
        total = term if total is None else total + term
    o_ref[...] = total.astype(o_ref.dtype)


def _merge(h, branches, w_gate, b_gate, w_branch):
    T, D = h.shape
    mix = branches[0].shape[1]
    tm = min(1024, T)
    tn = min(256, D)
    br = pl.BlockSpec((tm, mix), lambda i, j: (i, 0), pipeline_mode=pl.Buffered(1))
    need = 2 * (_nbytes((tm, D), BF16) + 3 * _nbytes((D, tn), BF16) + 3 * _nbytes((mix, tn), BF16)
                + _nbytes((tm, tn), BF16)) + 3 * _nbytes((tm, mix), BF16) + 10 * _nbytes((tm, tn), F32)
    return pl.pallas_call(
        _merge_kernel,
        out_shape=jax.ShapeDtypeStruct((T, D), BF16),
        grid=(T // tm, D // tn),
        in_specs=[
            pl.BlockSpec((tm, D), lambda i, j: (i, 0)),
            br, br, br,
            pl.BlockSpec((N_BRANCH, D, tn), lambda i, j: (0, 0, j)),
            pl.BlockSpec((N_BRANCH, 1, tn), lambda i, j: (0, 0, j)),
            pl.BlockSpec((N_BRANCH, mix, tn), lambda i, j: (0, 0, j)),
        ],
        out_specs=pl.BlockSpec((tm, tn), lambda i, j: (i, j)),
        compiler_params=_cparams(("parallel", "arbitrary"), need),
        name="branch_merge",
    )(h, *branches, w_gate, b_gate.reshape(N_BRANCH, 1, D), w_branch)


def _matmul_kernel(a_ref, w_ref, o_ref):
    o_ref[...] = jnp.dot(a_ref[...], w_ref[...], preferred_element_type=F32).astype(o_ref.dtype)


def _matmul(a, w, out_dtype, name):
    T, K = a.shape
    N = w.shape[1]
    tm = min(1024, T)
    tn = min(512, N)
    need = 2 * (_nbytes((tm, K), BF16) + _nbytes((K, tn), BF16) + _nbytes((tm, tn), out_dtype)) \
        + 2 * _nbytes((tm, tn), F32)
    return pl.pallas_call(
        _matmul_kernel,
        out_shape=jax.ShapeDtypeStruct((T, N), out_dtype),
        grid=(T // tm, N // tn),
        in_specs=[pl.BlockSpec((tm, K), lambda i, j: (i, 0)), pl.BlockSpec((K, tn), lambda i, j: (0, j))],
        out_specs=pl.BlockSpec((tm, tn), lambda i, j: (i, j)),
        compiler_params=_cparams(("parallel", "arbitrary"), need),
        name=name,
    )(a, w)


def _residual_ln_kernel(x_ref, y_ref, gate_ref, lg_ref, lb_ref, *rest, alpha, y_transposed, emit_h, emit_ht):
    y = y_ref[...].astype(F32)
    if y_transposed:
        y = y.T
    xn = _layer_norm_rows(alpha * x_ref[...] + gate_ref[...] * y, lg_ref[...], lb_ref[...])
    if emit_h:
        sc_ref, sh_ref, xo_ref, ho_ref = rest[:4]
        h = xn * (1.0 + sc_ref[...]) + sh_ref[...]
        ho_ref[...] = h.astype(ho_ref.dtype)
        if emit_ht:
            rest[4][...] = h.T.astype(ho_ref.dtype)
    else:
        (xo_ref,) = rest
    xo_ref[...] = xn


def _residual_ln(x, y, modr, gate_row, ln_g, ln_b, alpha, *, y_transposed, next_rows=None, emit_ht=False):
    B, S, D = x.shape
    ts = min(256, S)
    nst = S // ts
    emit_h = next_rows is not None

    def mod_spec(row_fn):
        return pl.BlockSpec((None, 1, D), lambda b, s: (row_fn(b), 0, 0))

    vec = pl.BlockSpec((1, D), lambda b, s: (0, 0))
    if y_transposed:
        y_spec = pl.BlockSpec((D, ts), lambda b, s: (0, b * nst + s))
    else:
        y_spec = pl.BlockSpec((ts, D), lambda b, s: (b * nst + s, 0))
    x_spec = pl.BlockSpec((None, ts, D), lambda b, s: (b, s, 0))
    in_specs = [x_spec, y_spec, mod_spec(gate_row), vec, vec]
    args = [x, y, modr, ln_g.reshape(1, D), ln_b.reshape(1, D)]
    out_shape = [jax.ShapeDtypeStruct((B, S, D), F32)]
    out_specs = [x_spec]
    if emit_h:
        in_specs += [mod_spec(next_rows[0]), mod_spec(next_rows[1])]
        args += [modr, modr]
        out_shape.append(jax.ShapeDtypeStruct((B * S, D), BF16))
        out_specs.append(pl.BlockSpec((ts, D), lambda b, s: (b * nst + s, 0)))
        if emit_ht:
            out_shape.append(jax.ShapeDtypeStruct((D, B * S), BF16))
            out_specs.append(pl.BlockSpec((D, ts), lambda b, s: (0, b * nst + s)))
    res = pl.pallas_call(
        functools.partial(_residual_ln_kernel, alpha=alpha, y_transposed=y_transposed, emit_h=emit_h,
                          emit_ht=emit_ht),
        out_shape=tuple(out_shape),
        grid=(B, nst),
        in_specs=in_specs,
        out_specs=tuple(out_specs),
        compiler_params=_cparams(("parallel", "parallel"), 14 * _nbytes((ts, D), F32)),
        name="residual_ln",
    )(*args)
    return tuple(res) + (None,) * (3 - len(res))


def _peer_scores_kernel(h_ref, wpq_ref, keys_ref, st_ref):
    q = jnp.dot(h_ref[...], wpq_ref[...], preferred_element_type=F32).astype(BF16)
    st_ref[...] = lax.dot_general(keys_ref[...], q, _NT, preferred_element_type=F32)


def _peer_scores(h, w_pq, keys_bd):
    T, D = h.shape
    HQ = w_pq.shape[1]
    R = keys_bd.shape[0]
    tm = min(512, T)
    need = 2 * (_nbytes((tm, D), BF16) + _nbytes((D, HQ), BF16) + _nbytes((R, HQ), BF16)
                + _nbytes((R, tm), F32)) + 2 * _nbytes((tm, HQ), F32)
    return pl.pallas_call(
        _peer_scores_kernel,
        out_shape=jax.ShapeDtypeStruct((R, T), F32),
        grid=(T // tm,),
        in_specs=[
            pl.BlockSpec((tm, D), lambda i: (i, 0)),
            pl.BlockSpec((D, HQ), lambda i: (0, 0)),
            pl.BlockSpec((R, HQ), lambda i: (0, 0)),
        ],
        out_specs=pl.BlockSpec((R, tm), lambda i: (0, i)),
        compiler_params=_cparams(("parallel",), need),
        name="peer_scores",
    )(h, w_pq, keys_bd)


def _top16_ranks(s, row_ids, exact):
    n = s.shape[0]
    k_ids = lax.broadcasted_iota(jnp.int32, (PEER_TOPK, s.shape[1]), 0)
    rank = jnp.full(s.shape, float(PEER_TOPK), F32)
    tops = jnp.zeros((PEER_TOPK, s.shape[1]), F32)
    for k in range(PEER_TOPK):
        m = jnp.max(s, axis=0, keepdims=True)
        sel = s == m
        if exact:
            first = jnp.min(jnp.where(sel, row_ids, float(n)), axis=0, keepdims=True)
            sel = row_ids == first
        rank = jnp.where(sel, float(k), rank)
        s = jnp.where(sel, -jnp.inf, s)
        tops = jnp.where(k_ids == k, m, tops)
    count = jnp.sum(jnp.where(rank < float(PEER_TOPK), 1.0, 0.0), axis=0, keepdims=True)
    return rank, tops, count


def _pair_counts(t1, t2, exact):
    L = t1.shape[1]
    K = PEER_TOPK
    sub = 8
    pieces = [t1[0:1] + t2]
    pos = [lax.broadcasted_iota(jnp.int32, (K, L), 0).astype(F32)]
    r8 = lax.broadcasted_iota(jnp.int32, (sub, L), 0).astype(F32)
    for a in range(1, sub):
        pieces.append(t1[a:a + 1] + t2[0:sub])
        pos.append(r8 + float(a * K))
    pieces.append(t1[sub:K] + t2[0:1])
    pos.append((r8 + float(sub)) * float(K))
    v = jnp.concatenate(pieces, axis=0)
    p = jnp.concatenate(pos, axis=0)
    picked = jnp.zeros(v.shape, F32)
    z = jnp.zeros((1, L), F32)
    top = None
    for k in range(K):
        m = jnp.max(v, axis=0, keepdims=True)
        sel = v == m
        if exact:
            first = jnp.min(jnp.where(sel, p, float(K * K)), axis=0, keepdims=True)
            sel = p == first
        picked = jnp.where(sel, 1.0, picked)
        v = jnp.where(sel, -jnp.inf, v)
        if k == 0:
            top = m
            z = z + 1.0
        else:
            z = z + jnp.exp(m - top)
    a_ids = lax.broadcasted_iota(jnp.int32, (sub, L), 0)
    low = jnp.zeros((sub, L), F32)
    low = jnp.where(a_ids == 0, jnp.sum(picked[0:K], axis=0, keepdims=True), low)
    for a in range(1, sub):
        off = K + (a - 1) * sub
        low = jnp.where(a_ids == a, jnp.sum(picked[off:off + sub], axis=0, keepdims=True), low)
    counts = jnp.concatenate([low, picked[K + (sub - 1) * sub:]], axis=0)
    return counts, z, jnp.sum(picked, axis=0, keepdims=True)


def _peer_select_kernel(st_ref, rank2_ref, cnt1_ref, e1_ref, e2_ref, *, heads, nk):
    tt = st_ref.shape[1]
    row_ids = lax.broadcasted_iota(jnp.int32, (nk, V7X_LANES), 0).astype(F32)
    full = float(PEER_TOPK)

    def per_head(idx, carry):
        c = idx // heads
        hh = idx % heads
        lanes = pl.ds(pl.multiple_of(c * V7X_LANES, V7X_LANES), V7X_LANES)
        r1 = pl.ds(pl.multiple_of((2 * hh) * nk, nk), nk)
        r2 = pl.ds(pl.multiple_of((2 * hh + 1) * nk, nk), nk)
        ro = pl.ds(pl.multiple_of(hh * nk, nk), nk)

        def select(exact):
            s1 = st_ref[r1, lanes]
            s2 = st_ref[r2, lanes]
            rank1, t1, n1 = _top16_ranks(s1, row_ids, exact)
            rank2, t2, n2 = _top16_ranks(s2, row_ids, exact)
            counts, z, n3 = _pair_counts(t1, t2, exact)
            cnt1 = jnp.zeros((nk, V7X_LANES), F32)
            for a in range(PEER_TOPK):
                cnt1 = jnp.where(rank1 == float(a), counts[a:a + 1], cnt1)
            rank2_ref[ro, lanes] = rank2.astype(rank2_ref.dtype)
            cnt1_ref[ro, lanes] = cnt1
            e1_ref[ro, lanes] = jnp.exp(s1 - t1[0:1]) / z
            e2_ref[ro, lanes] = jnp.exp(s2 - t2[0:1]).astype(e2_ref.dtype)
            return jnp.where((n1 == full) & (n2 == full) & (n3 == full), 0.0, 1.0)

        tied = jnp.max(select(exact=False)) > 0.0

        @pl.when(tied)
        def _():
            select(exact=True)

        return carry

    lax.fori_loop(0, (tt // V7X_LANES) * heads, per_head, 0)


def _peer_select(st, heads, nk):
    R, T = st.shape
    tt = min(512, T)
    blk = pl.BlockSpec((heads * nk, tt), lambda i: (0, i))
    return pl.pallas_call(
        functools.partial(_peer_select_kernel, heads=heads, nk=nk),
        out_shape=tuple(jax.ShapeDtypeStruct((heads * nk, T), dt) for dt in (BF16, F32, F32, BF16)),
        grid=(T // tt,),
        in_specs=[pl.BlockSpec((R, tt), lambda i: (0, i))],
        out_specs=(blk,) * 4,
        compiler_params=_cparams(("parallel",), 2 * _nbytes((R, tt), F32) + 8 * _nbytes((heads * nk, tt), F32)),
        name="peer_select",
    )(st)


def _peer_dense_kernel(ht_ref, wu_ref, wv_ref, rank2_ref, cnt1_ref, e1_ref, e2_ref, yt_ref,
                       a_ref, a1_ref, *, heads, nk, keys_per_step, n_tiles):
    e = pl.program_id(1)

    @pl.when(e == 0)
    def _():
        yt_ref[...] = jnp.zeros_like(yt_ref)
        a1_ref[...] = jnp.zeros_like(a1_ref)

    tile = jnp.minimum(e, n_tiles - 1)
    half = wu_ref.shape[0] // 2
    keys_per_half = keys_per_step // 2
    ht = ht_ref[...]
    acts = [jnp.dot(wu_ref[s * half:(s + 1) * half, :], ht, preferred_element_type=F32)
            for s in range(2)]

    tm = ht.shape[1]
    pk = V7X_BF16_SUBLANES
    zero = jnp.zeros((), BF16)

    def gated(s, dst_ref, row0):
        for b in range(keys_per_half):
            i1 = tile * keys_per_step + s * keys_per_half + b
            gate = None
            for hh in range(heads):
                row = pl.ds(hh * nk + i1, 1)
                blk = slice(hh * nk, (hh + 1) * nk)
                cnt = jnp.broadcast_to(cnt1_ref[row, :], (pk, tm)).astype(BF16)[None]
                e1 = jnp.broadcast_to(e1_ref[row, :], (pk, tm)).astype(BF16)[None]
                r2 = rank2_ref[blk, :].reshape(nk // pk, pk, tm)
                e2 = e2_ref[blk, :].reshape(nk // pk, pk, tm)
                term = jnp.where(r2 < cnt, e2 * e1, zero)
                gate = term if gate is None else gate + term
            src = slice(b * nk, (b + 1) * nk)
            dst = slice(row0 + b * nk, row0 + (b + 1) * nk)
            dst_ref[dst, :] = jax.nn.gelu(acts[s][src, :]).astype(BF16) * gate.reshape(nk, tm)

    a_ref[0:half, :] = a1_ref[...]
    gated(0, a_ref, half)
    yt_ref[...] += jnp.dot(wv_ref[...], a_ref[...], preferred_element_type=F32)
    gated(1, a1_ref, 0)


def _peer_dense(ht, w_u, w_vt_padded, sel, heads, nk):
    D, T = ht.shape
    E = w_u.shape[0]
    tm = min(512, T)
    te = _peer_tile(nk)
    keys_per_step = te // nk
    half = te // 2
    n_tiles = E // te
    sel_spec = pl.BlockSpec((heads * nk, tm), lambda i, e: (0, i), pipeline_mode=pl.Buffered(1))
    need = _nbytes((tm, D), BF16) + 4 * _nbytes((te, D), BF16) + 2 * _nbytes((D, tm), F32) \
        + 4 * _nbytes((heads * nk, tm), F32) + 2 * _nbytes((te, tm), BF16) + 4 * _nbytes((te, tm), F32)
    return pl.pallas_call(
        functools.partial(_peer_dense_kernel, heads=heads, nk=nk, keys_per_step=keys_per_step, n_tiles=n_tiles),
        out_shape=jax.ShapeDtypeStruct((D, T), F32),
        grid=(T // tm, n_tiles + 1),
        in_specs=[
            pl.BlockSpec((D, tm), lambda i, e: (0, i), pipeline_mode=pl.Buffered(1)),
            pl.BlockSpec((te, D), lambda i, e: (jnp.minimum(e, n_tiles - 1), 0)),
            pl.BlockSpec((D, te), lambda i, e: (0, e)),
            sel_spec, sel_spec, sel_spec, sel_spec,
        ],
        out_specs=pl.BlockSpec((D, tm), lambda i, e: (0, i)),
        scratch_shapes=[pltpu.VMEM((te, tm), BF16), pltpu.VMEM((half, tm), BF16)],
        compiler_params=_cparams(("parallel", "arbitrary"), need),
        name="peer_dense",
    )(ht, w_u, w_vt_padded, *sel)


def _peer_tile(nk):
    return min(4, nk) * nk


def _mixer_sublayer(h, w_in, cos_t, sin_t, w_gate, b_gate, ln_v_g, ln_v_b, w_sp, b_sp, conv_w,
                    w_branch, w_o, seq):
    mix = ln_v_g.shape[0]
    heads = mix // HEAD_DIM
    w_ac = jnp.concatenate([w_in[:, :2 * mix], w_in[:, 11 * mix:]], axis=1).astype(BF16)
    w_qkv = w_in[:, 2 * mix:11 * mix].astype(BF16)
    proj_ac = _proj_ac(h, w_ac, mix)
    qkv = _proj_qkv(h, w_qkv, cos_t, sin_t, mix)
    branch_a = _gmlp(proj_ac, ln_v_g, ln_v_b, w_sp, b_sp, mix)
    outs, lses = [], []
    for g, (_, dilation) in enumerate(B_PATTERNS):
        o, lse = _dilated_attention(qkv, g, dilation, seq, heads)
        outs.append(o)
        lses.append(lse)
    branch_b = _combine_groups(outs, lses)
    branch_c = _short_conv(proj_ac, conv_w, seq, mix, 2 * mix)
    merged = _merge(h, (branch_a, branch_b, branch_c), w_gate, b_gate, w_branch)
    return _matmul(merged, w_o, BF16, "out_proj")


def _peer_sublayer(h, ht, w_pq, sub_keys, w_u, w_v):
    heads, _, nk, _ = sub_keys.shape
    keys_bd = jax.scipy.linalg.block_diag(
        *[sub_keys[hh, p] for hh in range(heads) for p in range(2)]).astype(BF16)
    pad = _peer_tile(nk) // 2
    w_vt_padded = jnp.pad(w_v.T.astype(BF16), ((0, 0), (pad, pad)))
    st = _peer_scores(h, w_pq.astype(BF16), keys_bd)
    sel = _peer_select(st, heads, nk)
    return _peer_dense(ht, w_u.astype(BF16), w_vt_padded, sel, heads, nk)


def kernel(x, c, positions, w_ada, b_ada, w_in, w_gate, b_gate, ln_v_g, ln_v_b, w_sp, b_sp,
           conv_w, w_branch, w_o, ln1_g, ln1_b, w_pq, sub_keys, w_u, w_v, ln2_g, ln2_b):
    B, S, D = x.shape
    depth = w_ada.shape[0]
    alpha = (2 * depth) ** 0.25
    assert S % (ATTN_BLOCK * B_PATTERNS[-1][1]) == 0

    cos_t, sin_t = _rope_tables(positions)
    modr = _adaln(c, w_ada, b_ada)

    def mod_row(layer, k):
        return lambda b: (layer * B + b) * 6 + k

    h = _modulate(x, modr, mod_row(0, 1), mod_row(0, 0))
    for l in range(depth):
        y = _mixer_sublayer(h, w_in[l], cos_t, sin_t, w_gate[l].astype(BF16), b_gate[l],
                            ln_v_g[l], ln_v_b[l], w_sp[l], b_sp[l], conv_w[l],
                            w_branch[l].astype(BF16), w_o[l].astype(BF16), S)
        x, h, ht = _residual_ln(x, y, modr, mod_row(l, 2), ln1_g[l], ln1_b[l], alpha, y_transposed=False,
                                next_rows=(mod_row(l, 4), mod_row(l, 3)), emit_ht=True)
        y_t = _peer_sublayer(h, ht, w_pq[l], sub_keys[l], w_u[l], w_v[l])
        nxt = (mod_row(l + 1, 1), mod_row(l + 1, 0)) if l + 1 < depth else None
        x, h, _ = _residual_ln(x, y_t, modr, mod_row(l, 5), ln2_g[l], ln2_b[l], alpha, y_transposed=True,
                               next_rows=nxt)
    return x
```

```python
import functools

import jax
import jax.numpy as jnp
from jax import lax
from jax.experimental import pallas as pl
from jax.experimental.pallas import tpu as pltpu

F32 = jnp.float32
BF16 = jnp.bfloat16

HEAD_DIM = 128
CHUNK = 128
ATTN_BLOCK = 128
ROT_DIM = HEAD_DIM // 4
ROPE_THETA = 500000.0
B_PATTERNS = ((128, 1), (512, 4), (2048, 16))
N_BRANCH = 3
PEER_TOPK = 16
LN_EPS = 1e-5
MASKED = -1e30

V7X_LANES = 128
V7X_BF16_SUBLANES = 16
V7X_VMEM_LIMIT_CAP = 58 * 2**20
SPILL_AND_TEMP_BYTES = 4 * 2**20
ATTN_VMEM_BUDGET = 32 * 2**20

def _cparams(semantics, vmem_bytes, flags=None):
    limit = max(vmem_bytes + SPILL_AND_TEMP_BYTES, 16 * 2**20)
    return pltpu.CompilerParams(
        dimension_semantics=semantics,
        vmem_limit_bytes=int(min(limit, V7X_VMEM_LIMIT_CAP)),
        flags=flags,
    )


def _nbytes(shape, dtype):
    n = 1
    for s in shape:
        n *= s
    return n * jnp.dtype(dtype).itemsize


def _layer_norm_rows(z, g, b):
    mu = jnp.mean(z, axis=-1, keepdims=True)
    zc = z - mu
    var = jnp.mean(zc * zc, axis=-1, keepdims=True)
    return zc * lax.rsqrt(var + LN_EPS) * g + b


_NT = (((1,), (1,)), ((), ()))


def _rope_table_kernel(pos_ref, freq_ref, sign_ref, cos_ref, sin_ref):
    ang = pos_ref[...].astype(F32) * freq_ref[...]
    cos_ref[...] = jnp.cos(ang)
    sin_ref[...] = jnp.sin(ang) * sign_ref[...]


def _rope_tables(positions):
    T = positions.size
    half = ROT_DIM // 2
    inv_freq = ROPE_THETA ** (-jnp.arange(half, dtype=F32) / half)
    zeros = jnp.zeros((HEAD_DIM - ROT_DIM,), F32)
    freq = jnp.concatenate([inv_freq, inv_freq, zeros])[None, :]
    sign = jnp.concatenate([-jnp.ones((half,), F32), jnp.ones((half,), F32), zeros])[None, :]
    tm = min(1024, T)
    row = pl.BlockSpec((1, HEAD_DIM), lambda i: (0, 0))
    tab = pl.BlockSpec((tm, HEAD_DIM), lambda i: (i, 0))
    return pl.pallas_call(
        _rope_table_kernel,
        out_shape=(jax.ShapeDtypeStruct((T, HEAD_DIM), F32),) * 2,
        grid=(T // tm,),
        in_specs=[pl.BlockSpec((tm, 1), lambda i: (i, 0)), row, row],
        out_specs=(tab, tab),
        compiler_params=_cparams(("parallel",), 8 * _nbytes((tm, HEAD_DIM), F32)),
        name="rope_table",
    )(positions.reshape(T, 1), freq, sign)


def _adaln_kernel(c_ref, w_ref, b_ref, o_ref):
    c = c_ref[...]
    o_ref[...] = jnp.dot(c * jax.nn.sigmoid(c), w_ref[...], preferred_element_type=F32) + b_ref[...]


def _adaln(c, w_ada, b_ada):
    L, D, N = w_ada.shape
    B = c.shape[0]
    rows = 8
    c_pad = jnp.zeros((rows, D), F32).at[:B].set(c)
    tn = min(512, N)
    mod = pl.pallas_call(
        _adaln_kernel,
        out_shape=jax.ShapeDtypeStruct((L, rows, N), F32),
        grid=(L, N // tn),
        in_specs=[
            pl.BlockSpec((rows, D), lambda l, j: (0, 0)),
            pl.BlockSpec((None, D, tn), lambda l, j: (l, 0, j)),
            pl.BlockSpec((None, 1, tn), lambda l, j: (l, 0, j)),
        ],
        out_specs=pl.BlockSpec((None, rows, tn), lambda l, j: (l, 0, j)),
        compiler_params=_cparams(("parallel", "parallel"), 2 * _nbytes((D, tn), F32) + 2**22),
        name="adaln",
    )(c_pad, w_ada, b_ada.reshape(L, 1, N))
    return mod[:, :B].reshape(L * B * 6, 1, D)


def _modulate_kernel(x_ref, sc_ref, sh_ref, o_ref):
    o_ref[...] = (x_ref[...] * (1.0 + sc_ref[...]) + sh_ref[...]).astype(o_ref.dtype)


def _modulate(x, modr, sc_row, sh_row):
    B, S, D = x.shape
    ts = min(512, S)
    nst = S // ts
    return pl.pallas_call(
        _modulate_kernel,
        out_shape=jax.ShapeDtypeStruct((B * S, D), BF16),
        grid=(B, nst),
        in_specs=[
            pl.BlockSpec((None, ts, D), lambda b, s: (b, s, 0)),
            pl.BlockSpec((None, 1, D), lambda b, s: (sc_row(b), 0, 0)),
            pl.BlockSpec((None, 1, D), lambda b, s: (sh_row(b), 0, 0)),
        ],
        out_specs=pl.BlockSpec((ts, D), lambda b, s: (b * nst + s, 0)),
        compiler_params=_cparams(("parallel", "parallel"), 3 * _nbytes((ts, D), F32) + 2**22),
        name="modulate",
    )(x, modr, modr)


M_SPLIT = 2


def _row_chunks(tm):
    rows = tm // M_SPLIT
    return [slice(c * rows, (c + 1) * rows) for c in range(M_SPLIT)]


def _cast_weights_once(w_ref, wb_ref):
    @pl.when(pl.program_id(1) == 0)
    def _():
        wb_ref[...] = w_ref[...].astype(wb_ref.dtype)


def _proj_ac_kernel(a_ref, w_ref, o_ref, wb_ref, *, n_gelu):
    j = pl.program_id(0)
    _cast_weights_once(w_ref, wb_ref)

    @pl.when(j < n_gelu)
    def _():
        for rs in _row_chunks(a_ref.shape[0]):
            acc = jnp.dot(a_ref[rs, :], wb_ref[...], preferred_element_type=F32)
            o_ref[rs, :] = jax.nn.gelu(acc).astype(o_ref.dtype)

    @pl.when(j >= n_gelu)
    def _():
        for rs in _row_chunks(a_ref.shape[0]):
            acc = jnp.dot(a_ref[rs, :], wb_ref[...], preferred_element_type=F32)
            o_ref[rs, :] = acc.astype(o_ref.dtype)


def _proj_vmem(tm, D, tn, out_dtype):
    return 2 * (_nbytes((tm, D), BF16) + _nbytes((D, tn), F32) + _nbytes((tm, tn), out_dtype)) \
        + _nbytes((D, tn), BF16) + 4 * _nbytes((tm, tn), F32)


def _proj_ac(h, w_in, layer, mix):
    T, D = h.shape
    tm = min(1024, T)
    tn = min(512, mix)
    n_gelu = 2 * mix // tn
    skip = 9 * mix // tn

    def w_col(j, i):
        return (layer, 0, jnp.where(j < n_gelu, j, j + skip))

    return pl.pallas_call(
        functools.partial(_proj_ac_kernel, n_gelu=n_gelu),
        out_shape=jax.ShapeDtypeStruct((T, 5 * mix), BF16),
        grid=(5 * mix // tn, T // tm),
        in_specs=[pl.BlockSpec((tm, D), lambda j, i: (i, 0)), pl.BlockSpec((None, D, tn), w_col)],
        out_specs=pl.BlockSpec((tm, tn), lambda j, i: (i, j)),
        scratch_shapes=[pltpu.VMEM((D, tn), BF16)],
        compiler_params=_cparams(("arbitrary", "arbitrary"), _proj_vmem(tm, D, tn, BF16)),
        name="proj_ac",
    )(h, w_in)


def _store_heads(o_ref, rs, val):
    for hh in range(val.shape[1] // HEAD_DIM):
        o_ref[hh, rs, :] = val[:, hh * HEAD_DIM:(hh + 1) * HEAD_DIM]


def _proj_qkv_kernel(a_ref, w_ref, cos_ref, sin_ref, o_ref, wb_ref, *, n_rope):
    j = pl.program_id(0)
    tn = wb_ref.shape[1]
    _cast_weights_once(w_ref, wb_ref)

    @pl.when(j < n_rope)
    def _():
        reps = tn // HEAD_DIM
        half = ROT_DIM // 2
        for rs in _row_chunks(a_ref.shape[0]):
            acc = jnp.dot(a_ref[rs, :], wb_ref[...], preferred_element_type=F32)
            cos = jnp.concatenate([cos_ref[rs, :]] * reps, axis=1)
            sin = jnp.concatenate([sin_ref[rs, :]] * reps, axis=1)
            lane = lax.broadcasted_iota(jnp.int32, acc.shape, 1) % HEAD_DIM
            partner = jnp.where(lane < half, pltpu.roll(acc, tn - half, 1), pltpu.roll(acc, half, 1))
            _store_heads(o_ref, rs, acc * cos + partner * sin)

    @pl.when(j >= n_rope)
    def _():
        for rs in _row_chunks(a_ref.shape[0]):
            _store_heads(o_ref, rs, jnp.dot(a_ref[rs, :], wb_ref[...], preferred_element_type=F32))


def _proj_qkv(h, w_in, layer, cos_t, sin_t, mix):
    T, D = h.shape
    tm = min(1024, T)
    tn = min(512, mix)
    hpt = tn // HEAD_DIM
    first = 2 * mix // tn
    tab = pl.BlockSpec((tm, HEAD_DIM), lambda j, i: (i, 0))
    return pl.pallas_call(
        functools.partial(_proj_qkv_kernel, n_rope=6 * mix // tn),
        out_shape=jax.ShapeDtypeStruct((9 * mix // HEAD_DIM, T, HEAD_DIM), F32),
        grid=(9 * mix // tn, T // tm),
        in_specs=[
            pl.BlockSpec((tm, D), lambda j, i: (i, 0)),
            pl.BlockSpec((None, D, tn), lambda j, i: (layer, 0, first + j)),
            tab, tab,
        ],
        out_specs=pl.BlockSpec((hpt, tm, HEAD_DIM), lambda j, i: (j, i, 0)),
        scratch_shapes=[pltpu.VMEM((D, tn), BF16)],
        compiler_params=_cparams(("arbitrary", "arbitrary"),
                                 _proj_vmem(tm, D, tn, F32) + 4 * _nbytes((tm, HEAD_DIM), F32)),
        name="proj_qkv",
    )(h, w_in, cos_t, sin_t)


def _gmlp_kernel(u_ref, v_ref, g_ref, b_ref, wsp_ref, bspt_ref, o_ref, *, groups):
    tg = u_ref.shape[0]
    vn = _layer_norm_rows(v_ref[...].astype(F32), g_ref[...], b_ref[...]).astype(BF16)
    t_idx = lax.broadcasted_iota(jnp.int32, (CHUNK, CHUNK), 0)
    s_idx = lax.broadcasted_iota(jnp.int32, (CHUNK, CHUNK), 1)
    causal = s_idx <= t_idx
    for g in range(groups):
        cols = slice(g * CHUNK, (g + 1) * CHUNK)
        w = jnp.where(causal, wsp_ref[g], 0.0).astype(BF16)
        bias = bspt_ref[:, g:g + 1]
        for c in range(tg // CHUNK):
            rows = slice(c * CHUNK, (c + 1) * CHUNK)
            mixed = jnp.dot(w, vn[rows, cols], preferred_element_type=F32) + bias
            o_ref[rows, cols] = (u_ref[rows, cols].astype(F32) * mixed).astype(o_ref.dtype)


def _gmlp(proj, ln_g, ln_b, w_sp, b_sp, mix):
    T = proj.shape[0]
    groups = w_sp.shape[0]
    tg = min(512, T)
    row = pl.BlockSpec((1, mix), lambda i: (0, 0))
    return pl.pallas_call(
        functools.partial(_gmlp_kernel, groups=groups),
        out_shape=jax.ShapeDtypeStruct((T, mix), BF16),
        grid=(T // tg,),
        in_specs=[
            pl.BlockSpec((tg, mix), lambda i: (i, 0)),
            pl.BlockSpec((tg, mix), lambda i: (i, 1)),
            row, row,
            pl.BlockSpec((groups, CHUNK, CHUNK), lambda i: (0, 0, 0)),
            pl.BlockSpec((CHUNK, groups), lambda i: (0, 0)),
        ],
        out_specs=pl.BlockSpec((tg, mix), lambda i: (i, 0)),
        compiler_params=_cparams(("parallel",), 10 * _nbytes((tg, mix), F32)),
        name="gmlp",
    )(proj, proj, ln_g.reshape(1, mix), ln_b.reshape(1, mix), w_sp, b_sp.T)


def _attn_kernel(q_ref, kc_ref, kp_ref, vc_ref, vp_ref, o_ref, lse_ref, *, dilation, periods_per_seq):
    has_prev = (pl.program_id(0) % periods_per_seq) > 0
    qi = lax.broadcasted_iota(jnp.int32, (ATTN_BLOCK, ATTN_BLOCK), 0)
    kj = lax.broadcasted_iota(jnp.int32, (ATTN_BLOCK, ATTN_BLOCK), 1)
    mask_c = kj <= qi
    mask_p = (kj >= qi) & has_prev
    scale = HEAD_DIM ** -0.5

    def one_class(r, carry):
        rows = pl.ds(r, ATTN_BLOCK, stride=dilation) if dilation > 1 else pl.ds(0, ATTN_BLOCK)
        for h in range(q_ref.shape[0]):
            q = q_ref[h, rows, :].astype(BF16)
            s_c = lax.dot_general(q, kc_ref[h, rows, :].astype(BF16), _NT, preferred_element_type=F32) * scale
            s_p = lax.dot_general(q, kp_ref[h, rows, :].astype(BF16), _NT, preferred_element_type=F32) * scale
            s_c = jnp.where(mask_c, s_c, MASKED)
            s_p = jnp.where(mask_p, s_p, MASKED)
            m = jnp.maximum(jnp.max(s_c, axis=-1, keepdims=True), jnp.max(s_p, axis=-1, keepdims=True))
            p_c = jnp.exp(s_c - m)
            p_p = jnp.exp(s_p - m)
            l = jnp.sum(p_c, axis=-1, keepdims=True) + jnp.sum(p_p, axis=-1, keepdims=True)
            o = jnp.dot(p_c.astype(BF16), vc_ref[h, rows, :].astype(BF16), preferred_element_type=F32)
            o = o + jnp.dot(p_p.astype(BF16), vp_ref[h, rows, :].astype(BF16), preferred_element_type=F32)
            o_ref[h, rows, :] = o / l
            lse_ref[h, rows, :] = jnp.broadcast_to(m + jnp.log(l), (ATTN_BLOCK, HEAD_DIM))
        return carry

    if dilation > 1:
        lax.fori_loop(0, dilation, one_class, 0)
    else:
        one_class(0, 0)


def _dilated_attention(qkv, group, dilation, seq, heads):
    T = qkv.shape[1]
    period = ATTN_BLOCK * dilation
    periods_per_seq = seq // period
    hps = heads
    while 14 * _nbytes((hps, period, HEAD_DIM), F32) > ATTN_VMEM_BUDGET and hps % 2 == 0:
        hps //= 2

    def cur(part):
        return pl.BlockSpec((hps, period, HEAD_DIM), lambda p, h: ((3 * part + group) * (heads // hps) + h, p, 0))

    def prev(part):
        return pl.BlockSpec((hps, period, HEAD_DIM),
                            lambda p, h: ((3 * part + group) * (heads // hps) + h, jnp.maximum(p - 1, 0), 0))

    out = pl.BlockSpec((hps, period, HEAD_DIM), lambda p, h: (h, p, 0))
    shape = jax.ShapeDtypeStruct((heads, T, HEAD_DIM), F32)
    return pl.pallas_call(
        functools.partial(_attn_kernel, dilation=dilation, periods_per_seq=periods_per_seq),
        out_shape=(shape, shape),
        grid=(T // period, heads // hps),
        in_specs=[cur(0), cur(1), prev(1), cur(2), prev(2)],
        out_specs=(out, out),
        compiler_params=_cparams(("parallel", "parallel"), 14 * _nbytes((hps, period, HEAD_DIM), F32)),
        name=f"dilated_attention_d{dilation}",
    )(qkv, qkv, qkv, qkv, qkv)


def _combine_kernel(o0, o1, o2, l0, l1, l2, out_ref):
    for h in range(o0.shape[0]):
        a, b, c = l0[h], l1[h], l2[h]
        m = jnp.maximum(jnp.maximum(a, b), c)
        ea, eb, ec = jnp.exp(a - m), jnp.exp(b - m), jnp.exp(c - m)
        num = ea * o0[h] + eb * o1[h] + ec * o2[h]
        out_ref[:, h * HEAD_DIM:(h + 1) * HEAD_DIM] = (num / (ea + eb + ec)).astype(out_ref.dtype)


def _combine_groups(outs, lses):
    heads, T, _ = outs[0].shape
    tm = min(512, T)
    blk = pl.BlockSpec((heads, tm, HEAD_DIM), lambda i: (0, i, 0))
    return pl.pallas_call(
        _combine_kernel,
        out_shape=jax.ShapeDtypeStruct((T, heads * HEAD_DIM), BF16),
        grid=(T // tm,),
        in_specs=[blk] * 6,
        out_specs=pl.BlockSpec((tm, heads * HEAD_DIM), lambda i: (i, 0)),
        compiler_params=_cparams(("parallel",), 14 * _nbytes((heads, tm, HEAD_DIM), F32)),
        name="attn_combine",
    )(*outs, *lses)


def _conv_kernel(gb_ref, gc_ref, xin_ref, gcp_ref, xinp_ref, cw_ref, o_ref, *, tiles_per_seq):
    first = (pl.program_id(0) % tiles_per_seq) == 0
    z = gc_ref[...].astype(F32) * xin_ref[...].astype(F32)
    zp = gcp_ref[...].astype(F32) * xinp_ref[...].astype(F32)
    zp = jnp.where(first, 0.0, zp)
    last = zp.shape[0] - 1
    rows = lax.broadcasted_iota(jnp.int32, z.shape, 0)
    z1 = jnp.where(rows == 0, zp[last:last + 1], pltpu.roll(z, 1, 0))
    z2 = jnp.where(rows == 0, zp[last - 1:last], jnp.where(rows == 1, zp[last:last + 1], pltpu.roll(z, 2, 0)))
    y = cw_ref[0:1, :] * z2 + cw_ref[1:2, :] * z1 + cw_ref[2:3, :] * z
    o_ref[...] = (gb_ref[...].astype(F32) * y).astype(o_ref.dtype)


def _short_conv(proj, conv_w, seq, mix, col0):
    T = proj.shape[0]
    tc = min(512, seq)
    tiles_per_seq = seq // tc
    halo = V7X_BF16_SUBLANES
    cb = col0 // mix
    per_halo = tc // halo

    def cur(k):
        return pl.BlockSpec((tc, mix), lambda i: (i, cb + k))

    def prev(k):
        return pl.BlockSpec((halo, mix), lambda i: (jnp.maximum(i * per_halo - 1, 0), cb + k))

    return pl.pallas_call(
        functools.partial(_conv_kernel, tiles_per_seq=tiles_per_seq),
        out_shape=jax.ShapeDtypeStruct((T, mix), BF16),
        grid=(T // tc,),
        in_specs=[cur(0), cur(1), cur(2), prev(1), prev(2),
                  pl.BlockSpec(conv_w.shape, lambda i: (0, 0))],
        out_specs=pl.BlockSpec((tc, mix), lambda i: (i, 0)),
        compiler_params=_cparams(("parallel",), 16 * _nbytes((tc, mix), F32)),
        name="short_conv",
    )(proj, proj, proj, proj, proj, conv_w)


def _merge_kernel(h_ref, ba_ref, bb_ref, bc_ref, wg_ref, bg_ref, wb_ref, o_ref):
    h = h_ref[...]
    total = None
    for g, br in enumerate((ba_ref, bb_ref, bc_ref)):
        gate = jax.nn.sigmoid(jnp.dot(h, wg_ref[g], preferred_element_type=F32) + bg_ref[g])
        term = gate * jnp.dot(br[...], wb_ref[g], preferred_element_type=F32)
        total = term if total is None else total + term
    o_ref[...] = total.astype(o_ref.dtype)


def _merge(h, branches, w_gate, b_gate, w_branch):
    T, D = h.shape
    mix = branches[0].shape[1]
    tm = min(1024, T)
    tn = min(256, D)
    br = pl.BlockSpec((tm, mix), lambda i, j: (i, 0), pipeline_mode=pl.Buffered(1))
    need = 2 * (_nbytes((tm, D), BF16) + 3 * _nbytes((D, tn), BF16) + 3 * _nbytes((mix, tn), BF16)
                + _nbytes((tm, tn), BF16)) + 3 * _nbytes((tm, mix), BF16) + 10 * _nbytes((tm, tn), F32)
    return pl.pallas_call(
        _merge_kernel,
        out_shape=jax.ShapeDtypeStruct((T, D), BF16),
        grid=(T // tm, D // tn),
        in_specs=[
            pl.BlockSpec((tm, D), lambda i, j: (i, 0)),
            br, br, br,
            pl.BlockSpec((N_BRANCH, D, tn), lambda i, j: (0, 0, j)),
            pl.BlockSpec((N_BRANCH, 1, tn), lambda i, j: (0, 0, j)),
            pl.BlockSpec((N_BRANCH, mix, tn), lambda i, j: (0, 0, j)),
        ],
        out_specs=pl.BlockSpec((tm, tn), lambda i, j: (i, j)),
        compiler_params=_cparams(("parallel", "arbitrary"), need),
        name="branch_merge",
    )(h, *branches, w_gate, b_gate.reshape(N_BRANCH, 1, D), w_branch)


def _matmul_kernel(a_ref, w_ref, o_ref):
    o_ref[...] = jnp.dot(a_ref[...], w_ref[...], preferred_element_type=F32).astype(o_ref.dtype)


def _matmul(a, w, out_dtype, name):
    T, K = a.shape
    N = w.shape[1]
    tm = min(1024, T)
    tn = min(512, N)
    need = 2 * (_nbytes((tm, K), BF16) + _nbytes((K, tn), BF16) + _nbytes((tm, tn), out_dtype)) \
        + 2 * _nbytes((tm, tn), F32)
    return pl.pallas_call(
        _matmul_kernel,
        out_shape=jax.ShapeDtypeStruct((T, N), out_dtype),
        grid=(T // tm, N // tn),
        in_specs=[pl.BlockSpec((tm, K), lambda i, j: (i, 0)), pl.BlockSpec((K, tn), lambda i, j: (0, j))],
        out_specs=pl.BlockSpec((tm, tn), lambda i, j: (i, j)),
        compiler_params=_cparams(("parallel", "arbitrary"), need),
        name=name,
    )(a, w)


def _residual_ln_kernel(x_ref, y_ref, gate_ref, lg_ref, lb_ref, *rest, alpha, y_transposed, emit_h, emit_ht):
    y = y_ref[...].astype(F32)
    if y_transposed:
        y = y.T
    xn = _layer_norm_rows(alpha * x_ref[...] + gate_ref[...] * y, lg_ref[...], lb_ref[...])
    if emit_h:
        sc_ref, sh_ref, xo_ref, ho_ref = rest[:4]
        h = xn * (1.0 + sc_ref[...]) + sh_ref[...]
        ho_ref[...] = h.astype(ho_ref.dtype)
        if emit_ht:
            rest[4][...] = h.T.astype(ho_ref.dtype)
    else:
        (xo_ref,) = rest
    xo_ref[...] = xn


def _residual_ln(x, y, modr, gate_row, ln_g, ln_b, alpha, *, y_transposed, next_rows=None, emit_ht=False):
    B, S, D = x.shape
    ts = min(256, S)
    nst = S // ts
    emit_h = next_rows is not None

    def mod_spec(row_fn):
        return pl.BlockSpec((None, 1, D), lambda b, s: (row_fn(b), 0, 0))

    vec = pl.BlockSpec((1, D), lambda b, s: (0, 0))
    if y_transposed:
        y_spec = pl.BlockSpec((D, ts), lambda b, s: (0, b * nst + s))
    else:
        y_spec = pl.BlockSpec((ts, D), lambda b, s: (b * nst + s, 0))
    x_spec = pl.BlockSpec((None, ts, D), lambda b, s: (b, s, 0))
    in_specs = [x_spec, y_spec, mod_spec(gate_row), vec, vec]
    args = [x, y, modr, ln_g.reshape(1, D), ln_b.reshape(1, D)]
    out_shape = [jax.ShapeDtypeStruct((B, S, D), F32)]
    out_specs = [x_spec]
    if emit_h:
        in_specs += [mod_spec(next_rows[0]), mod_spec(next_rows[1])]
        args += [modr, modr]
        out_shape.append(jax.ShapeDtypeStruct((B * S, D), BF16))
        out_specs.append(pl.BlockSpec((ts, D), lambda b, s: (b * nst + s, 0)))
        if emit_ht:
            out_shape.append(jax.ShapeDtypeStruct((D, B * S), BF16))
            out_specs.append(pl.BlockSpec((D, ts), lambda b, s: (0, b * nst + s)))
    res = pl.pallas_call(
        functools.partial(_residual_ln_kernel, alpha=alpha, y_transposed=y_transposed, emit_h=emit_h,
                          emit_ht=emit_ht),
        out_shape=tuple(out_shape),
        grid=(B, nst),
        in_specs=in_specs,
        out_specs=tuple(out_specs),
        compiler_params=_cparams(("parallel", "parallel"), 14 * _nbytes((ts, D), F32)),
        name="residual_ln",
    )(*args)
    return tuple(res) + (None,) * (3 - len(res))


def _peer_scores_kernel(h_ref, wpq_ref, keys_ref, st_ref):
    q = jnp.dot(h_ref[...], wpq_ref[...], preferred_element_type=F32).astype(BF16)
    st_ref[...] = lax.dot_general(keys_ref[...], q, _NT, preferred_element_type=F32)


def _peer_scores(h, w_pq, keys_bd):
    T, D = h.shape
    HQ = w_pq.shape[1]
    R = keys_bd.shape[0]
    tm = min(512, T)
    need = 2 * (_nbytes((tm, D), BF16) + _nbytes((D, HQ), BF16) + _nbytes((R, HQ), BF16)
                + _nbytes((R, tm), F32)) + 2 * _nbytes((tm, HQ), F32)
    return pl.pallas_call(
        _peer_scores_kernel,
        out_shape=jax.ShapeDtypeStruct((R, T), F32),
        grid=(T // tm,),
        in_specs=[
            pl.BlockSpec((tm, D), lambda i: (i, 0)),
            pl.BlockSpec((D, HQ), lambda i: (0, 0)),
            pl.BlockSpec((R, HQ), lambda i: (0, 0)),
        ],
        out_specs=pl.BlockSpec((R, tm), lambda i: (0, i)),
        compiler_params=_cparams(("parallel",), need),
        name="peer_scores",
    )(h, w_pq, keys_bd)


def _top16_ranks(s, row_ids, exact):
    n = s.shape[0]
    k_ids = lax.broadcasted_iota(jnp.int32, (PEER_TOPK, s.shape[1]), 0)
    rank = jnp.full(s.shape, float(PEER_TOPK), F32)
    tops = jnp.zeros((PEER_TOPK, s.shape[1]), F32)
    for k in range(PEER_TOPK):
        m = jnp.max(s, axis=0, keepdims=True)
        sel = s == m
        if exact:
            first = jnp.min(jnp.where(sel, row_ids, float(n)), axis=0, keepdims=True)
            sel = row_ids == first
        rank = jnp.where(sel, float(k), rank)
        s = jnp.where(sel, -jnp.inf, s)
        tops = jnp.where(k_ids == k, m, tops)
    count = jnp.sum(jnp.where(rank < float(PEER_TOPK), 1.0, 0.0), axis=0, keepdims=True)
    return rank, tops, count


def _pair_counts(t1, t2, exact):
    L = t1.shape[1]
    K = PEER_TOPK
    sub = 8
    pieces = [t1[0:1] + t2]
    pos = [lax.broadcasted_iota(jnp.int32, (K, L), 0).astype(F32)]
    r8 = lax.broadcasted_iota(jnp.int32, (sub, L), 0).astype(F32)
    for a in range(1, sub):
        pieces.append(t1[a:a + 1] + t2[0:sub])
        pos.append(r8 + float(a * K))
    pieces.append(t1[sub:K] + t2[0:1])
    pos.append((r8 + float(sub)) * float(K))
    v = jnp.concatenate(pieces, axis=0)
    p = jnp.concatenate(pos, axis=0)
    picked = jnp.zeros(v.shape, F32)
    z = jnp.zeros((1, L), F32)
    top = None
    for k in range(K):
        m = jnp.max(v, axis=0, keepdims=True)
        sel = v == m
        if exact:
            first = jnp.min(jnp.where(sel, p, float(K * K)), axis=0, keepdims=True)
            sel = p == first
        picked = jnp.where(sel, 1.0, picked)
        v = jnp.where(sel, -jnp.inf, v)
        if k == 0:
            top = m
            z = z + 1.0
        else:
            z = z + jnp.exp(m - top)
    a_ids = lax.broadcasted_iota(jnp.int32, (sub, L), 0)
    low = jnp.zeros((sub, L), F32)
    low = jnp.where(a_ids == 0, jnp.sum(picked[0:K], axis=0, keepdims=True), low)
    for a in range(1, sub):
        off = K + (a - 1) * sub
        low = jnp.where(a_ids == a, jnp.sum(picked[off:off + sub], axis=0, keepdims=True), low)
    counts = jnp.concatenate([low, picked[K + (sub - 1) * sub:]], axis=0)
    return counts, z, jnp.sum(picked, axis=0, keepdims=True)


def _peer_select_kernel(st_ref, rank2_ref, cnt1_ref, e1_ref, e2_ref, *, heads, nk):
    tt = st_ref.shape[1]
    row_ids = lax.broadcasted_iota(jnp.int32, (nk, V7X_LANES), 0).astype(F32)
    full = float(PEER_TOPK)

    def per_head(idx, carry):
        c = idx // heads
        hh = idx % heads
        lanes = pl.ds(pl.multiple_of(c * V7X_LANES, V7X_LANES), V7X_LANES)
        r1 = pl.ds(pl.multiple_of((2 * hh) * nk, nk), nk)
        r2 = pl.ds(pl.multiple_of((2 * hh + 1) * nk, nk), nk)
        ro = pl.ds(pl.multiple_of(hh * nk, nk), nk)

        def select(exact):
            s1 = st_ref[r1, lanes]
            s2 = st_ref[r2, lanes]
            rank1, t1, n1 = _top16_ranks(s1, row_ids, exact)
            rank2, t2, n2 = _top16_ranks(s2, row_ids, exact)
            counts, z, n3 = _pair_counts(t1, t2, exact)
            cnt1 = jnp.zeros((nk, V7X_LANES), F32)
            for a in range(PEER_TOPK):
                cnt1 = jnp.where(rank1 == float(a), counts[a:a + 1], cnt1)
            rank2_ref[ro, lanes] = rank2.astype(rank2_ref.dtype)
            cnt1_ref[ro, lanes] = cnt1
            e1_ref[ro, lanes] = jnp.exp(s1 - t1[0:1]) / z
            e2_ref[ro, lanes] = jnp.exp(s2 - t2[0:1]).astype(e2_ref.dtype)
            return jnp.where((n1 == full) & (n2 == full) & (n3 == full), 0.0, 1.0)

        tied = jnp.max(select(exact=False)) > 0.0

        @pl.when(tied)
        def _():
            select(exact=True)

        return carry

    lax.fori_loop(0, (tt // V7X_LANES) * heads, per_head, 0)


def _peer_select(st, heads, nk):
    R, T = st.shape
    tt = min(512, T)
    blk = pl.BlockSpec((heads * nk, tt), lambda i: (0, i))
    return pl.pallas_call(
        functools.partial(_peer_select_kernel, heads=heads, nk=nk),
        out_shape=tuple(jax.ShapeDtypeStruct((heads * nk, T), dt) for dt in (BF16, F32, F32, BF16)),
        grid=(T // tt,),
        in_specs=[pl.BlockSpec((R, tt), lambda i: (0, i))],
        out_specs=(blk,) * 4,
        compiler_params=_cparams(("parallel",), 2 * _nbytes((R, tt), F32) + 8 * _nbytes((heads * nk, tt), F32)),
        name="peer_select",
    )(st)


def _peer_dense_kernel(ht_ref, wu_ref, wv_ref, rank2_ref, cnt1_ref, e1_ref, e2_ref, yt_ref,
                       a_ref, carry_ref, *, heads, nk, pieces, n_tiles):
    e = pl.program_id(1)

    @pl.when(e == 0)
    def _():
        yt_ref[...] = jnp.zeros_like(yt_ref)
        carry_ref[...] = jnp.zeros_like(carry_ref)

    keys_per_step = sum(pieces)
    tile = jnp.minimum(e, n_tiles - 1)
    ht = ht_ref[...]
    tm = ht.shape[1]
    pk = V7X_BF16_SUBLANES
    zero = jnp.zeros((), BF16)
    key0 = [sum(pieces[:p]) for p in range(len(pieces))]
    acts = [jnp.dot(wu_ref[key0[p] * nk:(key0[p] + pieces[p]) * nk, :], ht, preferred_element_type=F32)
            for p in range(len(pieces))]

    def gated(p, dst_ref, row0):
        for b in range(pieces[p]):
            i1 = tile * keys_per_step + key0[p] + b
            gate = None
            for hh in range(heads):
                row = pl.ds(hh * nk + i1, 1)
                blk = slice(hh * nk, (hh + 1) * nk)
                cnt = jnp.broadcast_to(cnt1_ref[row, :], (pk, tm)).astype(BF16)[None]
                e1 = jnp.broadcast_to(e1_ref[row, :], (pk, tm)).astype(BF16)[None]
                r2 = rank2_ref[blk, :].reshape(nk // pk, pk, tm)
                e2 = e2_ref[blk, :].reshape(nk // pk, pk, tm)
                term = jnp.where(r2 < cnt, e2 * e1, zero)
                gate = term if gate is None else gate + term
            src = slice(b * nk, (b + 1) * nk)
            dst = slice(row0 + b * nk, row0 + (b + 1) * nk)
            dst_ref[dst, :] = jax.nn.gelu(acts[p][src, :]).astype(BF16) * gate.reshape(nk, tm)

    carried = pieces[-1] * nk
    a_ref[0:carried, :] = carry_ref[...]
    for p in range(len(pieces) - 1):
        gated(p, a_ref, carried + key0[p] * nk)
    yt_ref[...] += jnp.dot(wv_ref[...], a_ref[...], preferred_element_type=F32)
    gated(len(pieces) - 1, carry_ref, 0)


def _peer_pieces(keys_per_step):
    if keys_per_step >= 4:
        return (keys_per_step - 2, 1, 1)
    return (keys_per_step - 1, 1)


def _peer_dense(ht, w_u, w_vt_padded, sel, heads, nk):
    D, T = ht.shape
    E = w_u.shape[0]
    tm = min(512, T)
    te = _peer_tile(nk)
    pieces = _peer_pieces(te // nk)
    n_tiles = E // te
    sel_spec = pl.BlockSpec((heads * nk, tm), lambda i, e: (0, i), pipeline_mode=pl.Buffered(1))
    need = _nbytes((tm, D), BF16) + 4 * _nbytes((te, D), BF16) + 2 * _nbytes((D, tm), F32) \
        + 4 * _nbytes((heads * nk, tm), F32) + 2 * _nbytes((te, tm), BF16) + 4 * _nbytes((te, tm), F32)
    return pl.pallas_call(
        functools.partial(_peer_dense_kernel, heads=heads, nk=nk, pieces=pieces, n_tiles=n_tiles),
        out_shape=jax.ShapeDtypeStruct((D, T), F32),
        grid=(T // tm, n_tiles + 1),
        in_specs=[
            pl.BlockSpec((D, tm), lambda i, e: (0, i), pipeline_mode=pl.Buffered(1)),
            pl.BlockSpec((te, D), lambda i, e: (jnp.minimum(e, n_tiles - 1), 0)),
            pl.BlockSpec((D, te), lambda i, e: (0, e)),
            sel_spec, sel_spec, sel_spec, sel_spec,
        ],
        out_specs=pl.BlockSpec((D, tm), lambda i, e: (0, i)),
        scratch_shapes=[pltpu.VMEM((te, tm), BF16), pltpu.VMEM((pieces[-1] * nk, tm), BF16)],
        compiler_params=_cparams(("parallel", "arbitrary"), need),
        name="peer_dense",
    )(ht, w_u, w_vt_padded, *sel)


def _peer_tile(nk):
    return min(4, nk) * nk


def _transpose_pad_kernel(w_ref, o_ref, *, n_blocks):
    j = pl.program_id(0)
    inside = (j > 0) & (j <= n_blocks)

    @pl.when(inside)
    def _():
        o_ref[...] = w_ref[...].T.astype(o_ref.dtype)

    @pl.when(jnp.logical_not(inside))
    def _():
        o_ref[...] = jnp.zeros_like(o_ref)


def _transpose_pad(w_v, front, back):
    E, D = w_v.shape
    n_blocks = E // front
    assert back % front == 0
    return pl.pallas_call(
        functools.partial(_transpose_pad_kernel, n_blocks=n_blocks),
        out_shape=jax.ShapeDtypeStruct((D, front + E + back), BF16),
        grid=(1 + n_blocks + back // front,),
        in_specs=[pl.BlockSpec((front, D), lambda j: (jnp.clip(j - 1, 0, n_blocks - 1), 0))],
        out_specs=pl.BlockSpec((D, front), lambda j: (0, j)),
        compiler_params=_cparams(("parallel",), 6 * _nbytes((front, D), F32)),
        name="transpose_pad_wv",
    )(w_v)


def _mixer_sublayer(h, w_in, layer, cos_t, sin_t, w_gate, b_gate, ln_v_g, ln_v_b, w_sp, b_sp, conv_w,
                    w_branch, w_o, seq):
    mix = ln_v_g.shape[0]
    heads = mix // HEAD_DIM
    proj_ac = _proj_ac(h, w_in, layer, mix)
    qkv = _proj_qkv(h, w_in, layer, cos_t, sin_t, mix)
    branch_a = _gmlp(proj_ac, ln_v_g, ln_v_b, w_sp, b_sp, mix)
    outs, lses = [], []
    for g, (_, dilation) in enumerate(B_PATTERNS):
        o, lse = _dilated_attention(qkv, g, dilation, seq, heads)
        outs.append(o)
        lses.append(lse)
    branch_b = _combine_groups(outs, lses)
    branch_c = _short_conv(proj_ac, conv_w, seq, mix, 2 * mix)
    merged = _merge(h, (branch_a, branch_b, branch_c), w_gate, b_gate, w_branch)
    return _matmul(merged, w_o, BF16, "out_proj")


def _peer_sublayer(h, ht, w_pq, sub_keys, w_u, w_v):
    heads, _, nk, _ = sub_keys.shape
    keys_bd = jax.scipy.linalg.block_diag(
        *[sub_keys[hh, p] for hh in range(heads) for p in range(2)]).astype(BF16)
    te = _peer_tile(nk)
    carried = _peer_pieces(te // nk)[-1] * nk
    w_vt_padded = _transpose_pad(w_v, carried, te - carried)
    st = _peer_scores(h, w_pq.astype(BF16), keys_bd)
    sel = _peer_select(st, heads, nk)
    return _peer_dense(ht, w_u.astype(BF16), w_vt_padded, sel, heads, nk)


def kernel(x, c, positions, w_ada, b_ada, w_in, w_gate, b_gate, ln_v_g, ln_v_b, w_sp, b_sp,
           conv_w, w_branch, w_o, ln1_g, ln1_b, w_pq, sub_keys, w_u, w_v, ln2_g, ln2_b):
    B, S, D = x.shape
    depth = w_ada.shape[0]
    alpha = (2 * depth) ** 0.25
    assert S % (ATTN_BLOCK * B_PATTERNS[-1][1]) == 0

    cos_t, sin_t = _rope_tables(positions)
    modr = _adaln(c, w_ada, b_ada)

    def mod_row(layer, k):
        return lambda b: (layer * B + b) * 6 + k

    h = _modulate(x, modr, mod_row(0, 1), mod_row(0, 0))
    for l in range(depth):
        y = _mixer_sublayer(h, w_in, l, cos_t, sin_t, w_gate[l].astype(BF16), b_gate[l],
                            ln_v_g[l], ln_v_b[l], w_sp[l], b_sp[l], conv_w[l],
                            w_branch[l].astype(BF16), w_o[l].astype(BF16), S)
        x, h, ht = _residual_ln(x, y, modr, mod_row(l, 2), ln1_g[l], ln1_b[l], alpha, y_transposed=False,
                                next_rows=(mod_row(l, 4), mod_row(l, 3)), emit_ht=True)
        y_t = _peer_sublayer(h, ht, w_pq[l], sub_keys[l], w_u[l], w_v[l])
        nxt = (mod_row(l + 1, 1), mod_row(l + 1, 0)) if l + 1 < depth else None
        x, h, _ = _residual_ln(x, y_t, modr, mod_row(l, 5), ln2_g[l], ln2_b[l], alpha, y_transposed=True,
                               next_rows=nxt)
    return x
```

```python
import functools

import jax
import jax.numpy as jnp
from jax import lax
from jax.experimental import pallas as pl
from jax.experimental.pallas import tpu as pltpu

F32 = jnp.float32
BF16 = jnp.bfloat16

HEAD_DIM = 128
CHUNK = 128
ATTN_BLOCK = 128
ROT_DIM = HEAD_DIM // 4
ROPE_THETA = 500000.0
B_PATTERNS = ((128, 1), (512, 4), (2048, 16))
N_BRANCH = 3
PEER_TOPK = 16
LN_EPS = 1e-5
MASKED = -1e30

V7X_LANES = 128
V7X_BF16_SUBLANES = 16
V7X_VMEM_LIMIT_CAP = 62 * 2**20
SPILL_AND_TEMP_BYTES = 4 * 2**20
ATTN_VMEM_BUDGET = 32 * 2**20

def _cparams(semantics, vmem_bytes, flags=None):
    limit = max(vmem_bytes + SPILL_AND_TEMP_BYTES, 16 * 2**20)
    return pltpu.CompilerParams(
        dimension_semantics=semantics,
        vmem_limit_bytes=int(min(limit, V7X_VMEM_LIMIT_CAP)),
        flags=flags,
    )


def _nbytes(shape, dtype):
    n = 1
    for s in shape:
        n *= s
    return n * jnp.dtype(dtype).itemsize


def _layer_norm_rows(z, g, b):
    mu = jnp.mean(z, axis=-1, keepdims=True)
    zc = z - mu
    var = jnp.mean(zc * zc, axis=-1, keepdims=True)
    return zc * lax.rsqrt(var + LN_EPS) * g + b


_NT = (((1,), (1,)), ((), ()))


def _rope_table_kernel(pos_ref, freq_ref, sign_ref, cos_ref, sin_ref):
    ang = pos_ref[...].astype(F32) * freq_ref[...]
    cos_ref[...] = jnp.cos(ang)
    sin_ref[...] = jnp.sin(ang) * sign_ref[...]


def _rope_tables(positions):
    T = positions.size
    half = ROT_DIM // 2
    inv_freq = ROPE_THETA ** (-jnp.arange(half, dtype=F32) / half)
    zeros = jnp.zeros((HEAD_DIM - ROT_DIM,), F32)
    freq = jnp.concatenate([inv_freq, inv_freq, zeros])[None, :]
    sign = jnp.concatenate([-jnp.ones((half,), F32), jnp.ones((half,), F32), zeros])[None, :]
    tm = min(1024, T)
    row = pl.BlockSpec((1, HEAD_DIM), lambda i: (0, 0))
    tab = pl.BlockSpec((tm, HEAD_DIM), lambda i: (i, 0))
    return pl.pallas_call(
        _rope_table_kernel,
        out_shape=(jax.ShapeDtypeStruct((T, HEAD_DIM), F32),) * 2,
        grid=(T // tm,),
        in_specs=[pl.BlockSpec((tm, 1), lambda i: (i, 0)), row, row],
        out_specs=(tab, tab),
        compiler_params=_cparams(("parallel",), 8 * _nbytes((tm, HEAD_DIM), F32)),
        name="rope_table",
    )(positions.reshape(T, 1), freq, sign)


def _adaln_kernel(c_ref, w_ref, b_ref, o_ref):
    c = c_ref[...]
    o_ref[...] = jnp.dot(c * jax.nn.sigmoid(c), w_ref[...], preferred_element_type=F32) + b_ref[...]


def _adaln(c, w_ada, b_ada):
    L, D, N = w_ada.shape
    B = c.shape[0]
    rows = 8
    c_pad = jnp.zeros((rows, D), F32).at[:B].set(c)
    tn = min(512, N)
    mod = pl.pallas_call(
        _adaln_kernel,
        out_shape=jax.ShapeDtypeStruct((L, rows, N), F32),
        grid=(L, N // tn),
        in_specs=[
            pl.BlockSpec((rows, D), lambda l, j: (0, 0)),
            pl.BlockSpec((None, D, tn), lambda l, j: (l, 0, j)),
            pl.BlockSpec((None, 1, tn), lambda l, j: (l, 0, j)),
        ],
        out_specs=pl.BlockSpec((None, rows, tn), lambda l, j: (l, 0, j)),
        compiler_params=_cparams(("parallel", "parallel"), 2 * _nbytes((D, tn), F32) + 2**22),
        name="adaln",
    )(c_pad, w_ada, b_ada.reshape(L, 1, N))
    return mod[:, :B].reshape(L * B * 6, 1, D)


def _modulate_kernel(x_ref, sc_ref, sh_ref, o_ref):
    o_ref[...] = (x_ref[...] * (1.0 + sc_ref[...]) + sh_ref[...]).astype(o_ref.dtype)


def _modulate(x, modr, sc_row, sh_row):
    B, S, D = x.shape
    ts = min(512, S)
    nst = S // ts
    return pl.pallas_call(
        _modulate_kernel,
        out_shape=jax.ShapeDtypeStruct((B * S, D), BF16),
        grid=(B, nst),
        in_specs=[
            pl.BlockSpec((None, ts, D), lambda b, s: (b, s, 0)),
            pl.BlockSpec((None, 1, D), lambda b, s: (sc_row(b), 0, 0)),
            pl.BlockSpec((None, 1, D), lambda b, s: (sh_row(b), 0, 0)),
        ],
        out_specs=pl.BlockSpec((ts, D), lambda b, s: (b * nst + s, 0)),
        compiler_params=_cparams(("parallel", "parallel"), 3 * _nbytes((ts, D), F32) + 2**22),
        name="modulate",
    )(x, modr, modr)


M_SPLIT = 2


def _row_chunks(tm):
    rows = tm // M_SPLIT
    return [slice(c * rows, (c + 1) * rows) for c in range(M_SPLIT)]


def _cast_weights_once(w_ref, wb_ref):
    @pl.when(pl.program_id(1) == 0)
    def _():
        wb_ref[...] = w_ref[...].astype(wb_ref.dtype)


def _proj_ac_kernel(a_ref, w_ref, o_ref, wb_ref, *, n_gelu):
    j = pl.program_id(0)
    _cast_weights_once(w_ref, wb_ref)

    @pl.when(j < n_gelu)
    def _():
        for rs in _row_chunks(a_ref.shape[0]):
            acc = jnp.dot(a_ref[rs, :], wb_ref[...], preferred_element_type=F32)
            o_ref[rs, :] = jax.nn.gelu(acc).astype(o_ref.dtype)

    @pl.when(j >= n_gelu)
    def _():
        for rs in _row_chunks(a_ref.shape[0]):
            acc = jnp.dot(a_ref[rs, :], wb_ref[...], preferred_element_type=F32)
            o_ref[rs, :] = acc.astype(o_ref.dtype)


def _proj_vmem(tm, D, tn, out_dtype):
    return 2 * (_nbytes((tm, D), BF16) + _nbytes((D, tn), F32) + _nbytes((tm, tn), out_dtype)) \
        + _nbytes((D, tn), BF16) + 4 * _nbytes((tm, tn), F32)


def _proj_ac(h, w_in, layer, mix):
    T, D = h.shape
    tm = min(1024, T)
    tn = min(512, mix)
    n_gelu = 2 * mix // tn
    skip = 9 * mix // tn

    def w_col(j, i):
        return (layer, 0, jnp.where(j < n_gelu, j, j + skip))

    return pl.pallas_call(
        functools.partial(_proj_ac_kernel, n_gelu=n_gelu),
        out_shape=jax.ShapeDtypeStruct((T, 5 * mix), BF16),
        grid=(5 * mix // tn, T // tm),
        in_specs=[pl.BlockSpec((tm, D), lambda j, i: (i, 0)), pl.BlockSpec((None, D, tn), w_col)],
        out_specs=pl.BlockSpec((tm, tn), lambda j, i: (i, j)),
        scratch_shapes=[pltpu.VMEM((D, tn), BF16)],
        compiler_params=_cparams(("arbitrary", "arbitrary"), _proj_vmem(tm, D, tn, BF16)),
        name="proj_ac",
    )(h, w_in)


def _store_heads(o_ref, rs, val):
    for hh in range(val.shape[1] // HEAD_DIM):
        o_ref[hh, rs, :] = val[:, hh * HEAD_DIM:(hh + 1) * HEAD_DIM]


def _proj_qkv_kernel(a_ref, w_ref, cos_ref, sin_ref, o_ref, wb_ref, *, n_rope):
    j = pl.program_id(0)
    tn = wb_ref.shape[1]
    _cast_weights_once(w_ref, wb_ref)

    @pl.when(j < n_rope)
    def _():
        reps = tn // HEAD_DIM
        half = ROT_DIM // 2
        for rs in _row_chunks(a_ref.shape[0]):
            acc = jnp.dot(a_ref[rs, :], wb_ref[...], preferred_element_type=F32)
            cos = jnp.concatenate([cos_ref[rs, :]] * reps, axis=1)
            sin = jnp.concatenate([sin_ref[rs, :]] * reps, axis=1)
            lane = lax.broadcasted_iota(jnp.int32, acc.shape, 1) % HEAD_DIM
            partner = jnp.where(lane < half, pltpu.roll(acc, tn - half, 1), pltpu.roll(acc, half, 1))
            _store_heads(o_ref, rs, acc * cos + partner * sin)

    @pl.when(j >= n_rope)
    def _():
        for rs in _row_chunks(a_ref.shape[0]):
            _store_heads(o_ref, rs, jnp.dot(a_ref[rs, :], wb_ref[...], preferred_element_type=F32))


def _proj_qkv(h, w_in, layer, cos_t, sin_t, mix):
    T, D = h.shape
    tm = min(1024, T)
    tn = min(512, mix)
    hpt = tn // HEAD_DIM
    first = 2 * mix // tn
    tab = pl.BlockSpec((tm, HEAD_DIM), lambda j, i: (i, 0))
    return pl.pallas_call(
        functools.partial(_proj_qkv_kernel, n_rope=6 * mix // tn),
        out_shape=jax.ShapeDtypeStruct((9 * mix // HEAD_DIM, T, HEAD_DIM), F32),
        grid=(9 * mix // tn, T // tm),
        in_specs=[
            pl.BlockSpec((tm, D), lambda j, i: (i, 0)),
            pl.BlockSpec((None, D, tn), lambda j, i: (layer, 0, first + j)),
            tab, tab,
        ],
        out_specs=pl.BlockSpec((hpt, tm, HEAD_DIM), lambda j, i: (j, i, 0)),
        scratch_shapes=[pltpu.VMEM((D, tn), BF16)],
        compiler_params=_cparams(("arbitrary", "arbitrary"),
                                 _proj_vmem(tm, D, tn, F32) + 4 * _nbytes((tm, HEAD_DIM), F32)),
        name="proj_qkv",
    )(h, w_in, cos_t, sin_t)


def _gmlp_kernel(u_ref, v_ref, g_ref, b_ref, wsp_ref, bspt_ref, o_ref, *, groups):
    tg = u_ref.shape[0]
    vn = _layer_norm_rows(v_ref[...].astype(F32), g_ref[...], b_ref[...]).astype(BF16)
    t_idx = lax.broadcasted_iota(jnp.int32, (CHUNK, CHUNK), 0)
    s_idx = lax.broadcasted_iota(jnp.int32, (CHUNK, CHUNK), 1)
    causal = s_idx <= t_idx
    for g in range(groups):
        cols = slice(g * CHUNK, (g + 1) * CHUNK)
        w = jnp.where(causal, wsp_ref[g], 0.0).astype(BF16)
        bias = bspt_ref[:, g:g + 1]
        for c in range(tg // CHUNK):
            rows = slice(c * CHUNK, (c + 1) * CHUNK)
            mixed = jnp.dot(w, vn[rows, cols], preferred_element_type=F32) + bias
            o_ref[rows, cols] = (u_ref[rows, cols].astype(F32) * mixed).astype(o_ref.dtype)


def _gmlp(proj, ln_g, ln_b, w_sp, b_sp, mix):
    T = proj.shape[0]
    groups = w_sp.shape[0]
    tg = min(512, T)
    row = pl.BlockSpec((1, mix), lambda i: (0, 0))
    return pl.pallas_call(
        functools.partial(_gmlp_kernel, groups=groups),
        out_shape=jax.ShapeDtypeStruct((T, mix), BF16),
        grid=(T // tg,),
        in_specs=[
            pl.BlockSpec((tg, mix), lambda i: (i, 0)),
            pl.BlockSpec((tg, mix), lambda i: (i, 1)),
            row, row,
            pl.BlockSpec((groups, CHUNK, CHUNK), lambda i: (0, 0, 0)),
            pl.BlockSpec((CHUNK, groups), lambda i: (0, 0)),
        ],
        out_specs=pl.BlockSpec((tg, mix), lambda i: (i, 0)),
        compiler_params=_cparams(("parallel",), 10 * _nbytes((tg, mix), F32)),
        name="gmlp",
    )(proj, proj, ln_g.reshape(1, mix), ln_b.reshape(1, mix), w_sp, b_sp.T)


def _attn_kernel(q_ref, kc_ref, kp_ref, vc_ref, vp_ref, o_ref, lse_ref, *, dilation, periods_per_seq):
    has_prev = (pl.program_id(0) % periods_per_seq) > 0
    qi = lax.broadcasted_iota(jnp.int32, (ATTN_BLOCK, ATTN_BLOCK), 0)
    kj = lax.broadcasted_iota(jnp.int32, (ATTN_BLOCK, ATTN_BLOCK), 1)
    mask_c = kj <= qi
    mask_p = (kj >= qi) & has_prev
    scale = HEAD_DIM ** -0.5

    def one_class(r, carry):
        rows = pl.ds(r, ATTN_BLOCK, stride=dilation) if dilation > 1 else pl.ds(0, ATTN_BLOCK)
        for h in range(q_ref.shape[0]):
            q = q_ref[h, rows, :].astype(BF16)
            s_c = lax.dot_general(q, kc_ref[h, rows, :].astype(BF16), _NT, preferred_element_type=F32) * scale
            s_p = lax.dot_general(q, kp_ref[h, rows, :].astype(BF16), _NT, preferred_element_type=F32) * scale
            s_c = jnp.where(mask_c, s_c, MASKED)
            s_p = jnp.where(mask_p, s_p, MASKED)
            m = jnp.maximum(jnp.max(s_c, axis=-1, keepdims=True), jnp.max(s_p, axis=-1, keepdims=True))
            p_c = jnp.exp(s_c - m)
            p_p = jnp.exp(s_p - m)
            l = jnp.sum(p_c, axis=-1, keepdims=True) + jnp.sum(p_p, axis=-1, keepdims=True)
            o = jnp.dot(p_c.astype(BF16), vc_ref[h, rows, :].astype(BF16), preferred_element_type=F32)
            o = o + jnp.dot(p_p.astype(BF16), vp_ref[h, rows, :].astype(BF16), preferred_element_type=F32)
            o_ref[h, rows, :] = o / l
            lse_ref[h, rows, :] = jnp.broadcast_to(m + jnp.log(l), (ATTN_BLOCK, HEAD_DIM))
        return carry

    if dilation > 1:
        lax.fori_loop(0, dilation, one_class, 0)
    else:
        one_class(0, 0)


def _dilated_attention(qkv, group, dilation, seq, heads):
    T = qkv.shape[1]
    period = ATTN_BLOCK * dilation
    periods_per_seq = seq // period
    hps = heads
    while 14 * _nbytes((hps, period, HEAD_DIM), F32) > ATTN_VMEM_BUDGET and hps % 2 == 0:
        hps //= 2

    def cur(part):
        return pl.BlockSpec((hps, period, HEAD_DIM), lambda p, h: ((3 * part + group) * (heads // hps) + h, p, 0))

    def prev(part):
        return pl.BlockSpec((hps, period, HEAD_DIM),
                            lambda p, h: ((3 * part + group) * (heads // hps) + h, jnp.maximum(p - 1, 0), 0))

    out = pl.BlockSpec((hps, period, HEAD_DIM), lambda p, h: (h, p, 0))
    shape = jax.ShapeDtypeStruct((heads, T, HEAD_DIM), F32)
    return pl.pallas_call(
        functools.partial(_attn_kernel, dilation=dilation, periods_per_seq=periods_per_seq),
        out_shape=(shape, shape),
        grid=(T // period, heads // hps),
        in_specs=[cur(0), cur(1), prev(1), cur(2), prev(2)],
        out_specs=(out, out),
        compiler_params=_cparams(("parallel", "parallel"), 14 * _nbytes((hps, period, HEAD_DIM), F32)),
        name=f"dilated_attention_d{dilation}",
    )(qkv, qkv, qkv, qkv, qkv)


def _combine_kernel(o0, o1, o2, l0, l1, l2, out_ref):
    for h in range(o0.shape[0]):
        a, b, c = l0[h], l1[h], l2[h]
        m = jnp.maximum(jnp.maximum(a, b), c)
        ea, eb, ec = jnp.exp(a - m), jnp.exp(b - m), jnp.exp(c - m)
        num = ea * o0[h] + eb * o1[h] + ec * o2[h]
        out_ref[:, h * HEAD_DIM:(h + 1) * HEAD_DIM] = (num / (ea + eb + ec)).astype(out_ref.dtype)


def _combine_groups(outs, lses):
    heads, T, _ = outs[0].shape
    tm = min(512, T)
    blk = pl.BlockSpec((heads, tm, HEAD_DIM), lambda i: (0, i, 0))
    return pl.pallas_call(
        _combine_kernel,
        out_shape=jax.ShapeDtypeStruct((T, heads * HEAD_DIM), BF16),
        grid=(T // tm,),
        in_specs=[blk] * 6,
        out_specs=pl.BlockSpec((tm, heads * HEAD_DIM), lambda i: (i, 0)),
        compiler_params=_cparams(("parallel",), 14 * _nbytes((heads, tm, HEAD_DIM), F32)),
        name="attn_combine",
    )(*outs, *lses)


def _conv_kernel(gb_ref, gc_ref, xin_ref, gcp_ref, xinp_ref, cw_ref, o_ref, *, tiles_per_seq):
    first = (pl.program_id(0) % tiles_per_seq) == 0
    z = gc_ref[...].astype(F32) * xin_ref[...].astype(F32)
    zp = gcp_ref[...].astype(F32) * xinp_ref[...].astype(F32)
    zp = jnp.where(first, 0.0, zp)
    last = zp.shape[0] - 1
    rows = lax.broadcasted_iota(jnp.int32, z.shape, 0)
    z1 = jnp.where(rows == 0, zp[last:last + 1], pltpu.roll(z, 1, 0))
    z2 = jnp.where(rows == 0, zp[last - 1:last], jnp.where(rows == 1, zp[last:last + 1], pltpu.roll(z, 2, 0)))
    y = cw_ref[0:1, :] * z2 + cw_ref[1:2, :] * z1 + cw_ref[2:3, :] * z
    o_ref[...] = (gb_ref[...].astype(F32) * y).astype(o_ref.dtype)


def _short_conv(proj, conv_w, seq, mix, col0):
    T = proj.shape[0]
    tc = min(512, seq)
    tiles_per_seq = seq // tc
    halo = V7X_BF16_SUBLANES
    cb = col0 // mix
    per_halo = tc // halo

    def cur(k):
        return pl.BlockSpec((tc, mix), lambda i: (i, cb + k))

    def prev(k):
        return pl.BlockSpec((halo, mix), lambda i: (jnp.maximum(i * per_halo - 1, 0), cb + k))

    return pl.pallas_call(
        functools.partial(_conv_kernel, tiles_per_seq=tiles_per_seq),
        out_shape=jax.ShapeDtypeStruct((T, mix), BF16),
        grid=(T // tc,),
        in_specs=[cur(0), cur(1), cur(2), prev(1), prev(2),
                  pl.BlockSpec(conv_w.shape, lambda i: (0, 0))],
        out_specs=pl.BlockSpec((tc, mix), lambda i: (i, 0)),
        compiler_params=_cparams(("parallel",), 16 * _nbytes((tc, mix), F32)),
        name="short_conv",
    )(proj, proj, proj, proj, proj, conv_w)


def _merge_kernel(h_ref, ba_ref, bb_ref, bc_ref, wg_ref, bg_ref, wb_ref, o_ref):
    h = h_ref[...]
    total = None
    for g, br in enumerate((ba_ref, bb_ref, bc_ref)):
        gate = jax.nn.sigmoid(jnp.dot(h, wg_ref[g], preferred_element_type=F32) + bg_ref[g])
        term = gate * jnp.dot(br[...], wb_ref[g], preferred_element_type=F32)
        total = term if total is None else total + term
    o_ref[...] = total.astype(o_ref.dtype)


def _merge(h, branches, w_gate, b_gate, w_branch):
    T, D = h.shape
    mix = branches[0].shape[1]
    tm = min(1024, T)
    tn = min(256, D)
    br = pl.BlockSpec((tm, mix), lambda i, j: (i, 0), pipeline_mode=pl.Buffered(1))
    need = 2 * (_nbytes((tm, D), BF16) + 3 * _nbytes((D, tn), BF16) + 3 * _nbytes((mix, tn), BF16)
                + _nbytes((tm, tn), BF16)) + 3 * _nbytes((tm, mix), BF16) + 10 * _nbytes((tm, tn), F32)
    return pl.pallas_call(
        _merge_kernel,
        out_shape=jax.ShapeDtypeStruct((T, D), BF16),
        grid=(T // tm, D // tn),
        in_specs=[
            pl.BlockSpec((tm, D), lambda i, j: (i, 0)),
            br, br, br,
            pl.BlockSpec((N_BRANCH, D, tn), lambda i, j: (0, 0, j)),
            pl.BlockSpec((N_BRANCH, 1, tn), lambda i, j: (0, 0, j)),
            pl.BlockSpec((N_BRANCH, mix, tn), lambda i, j: (0, 0, j)),
        ],
        out_specs=pl.BlockSpec((tm, tn), lambda i, j: (i, j)),
        compiler_params=_cparams(("parallel", "arbitrary"), need),
        name="branch_merge",
    )(h, *branches, w_gate, b_gate.reshape(N_BRANCH, 1, D), w_branch)


def _matmul_kernel(a_ref, w_ref, o_ref):
    o_ref[...] = jnp.dot(a_ref[...], w_ref[...], preferred_element_type=F32).astype(o_ref.dtype)


def _matmul(a, w, out_dtype, name):
    T, K = a.shape
    N = w.shape[1]
    tm = min(1024, T)
    tn = min(512, N)
    need = 2 * (_nbytes((tm, K), BF16) + _nbytes((K, tn), BF16) + _nbytes((tm, tn), out_dtype)) \
        + 2 * _nbytes((tm, tn), F32)
    return pl.pallas_call(
        _matmul_kernel,
        out_shape=jax.ShapeDtypeStruct((T, N), out_dtype),
        grid=(T // tm, N // tn),
        in_specs=[pl.BlockSpec((tm, K), lambda i, j: (i, 0)), pl.BlockSpec((K, tn), lambda i, j: (0, j))],
        out_specs=pl.BlockSpec((tm, tn), lambda i, j: (i, j)),
        compiler_params=_cparams(("parallel", "arbitrary"), need),
        name=name,
    )(a, w)


def _residual_ln_kernel(x_ref, y_ref, gate_ref, lg_ref, lb_ref, *rest, alpha, y_transposed, emit_h, emit_ht):
    y = y_ref[...].astype(F32)
    if y_transposed:
        y = y.T
    xn = _layer_norm_rows(alpha * x_ref[...] + gate_ref[...] * y, lg_ref[...], lb_ref[...])
    if emit_h:
        sc_ref, sh_ref, xo_ref, ho_ref = rest[:4]
        h = xn * (1.0 + sc_ref[...]) + sh_ref[...]
        ho_ref[...] = h.astype(ho_ref.dtype)
        if emit_ht:
            rest[4][...] = h.T.astype(ho_ref.dtype)
    else:
        (xo_ref,) = rest
    xo_ref[...] = xn


def _residual_ln(x, y, modr, gate_row, ln_g, ln_b, alpha, *, y_transposed, next_rows=None, emit_ht=False):
    B, S, D = x.shape
    ts = min(256, S)
    nst = S // ts
    emit_h = next_rows is not None

    def mod_spec(row_fn):
        return pl.BlockSpec((None, 1, D), lambda b, s: (row_fn(b), 0, 0))

    vec = pl.BlockSpec((1, D), lambda b, s: (0, 0))
    if y_transposed:
        y_spec = pl.BlockSpec((D, ts), lambda b, s: (0, b * nst + s))
    else:
        y_spec = pl.BlockSpec((ts, D), lambda b, s: (b * nst + s, 0))
    x_spec = pl.BlockSpec((None, ts, D), lambda b, s: (b, s, 0))
    in_specs = [x_spec, y_spec, mod_spec(gate_row), vec, vec]
    args = [x, y, modr, ln_g.reshape(1, D), ln_b.reshape(1, D)]
    out_shape = [jax.ShapeDtypeStruct((B, S, D), F32)]
    out_specs = [x_spec]
    if emit_h:
        in_specs += [mod_spec(next_rows[0]), mod_spec(next_rows[1])]
        args += [modr, modr]
        out_shape.append(jax.ShapeDtypeStruct((B * S, D), BF16))
        out_specs.append(pl.BlockSpec((ts, D), lambda b, s: (b * nst + s, 0)))
        if emit_ht:
            out_shape.append(jax.ShapeDtypeStruct((D, B * S), BF16))
            out_specs.append(pl.BlockSpec((D, ts), lambda b, s: (0, b * nst + s)))
    res = pl.pallas_call(
        functools.partial(_residual_ln_kernel, alpha=alpha, y_transposed=y_transposed, emit_h=emit_h,
                          emit_ht=emit_ht),
        out_shape=tuple(out_shape),
        grid=(B, nst),
        in_specs=in_specs,
        out_specs=tuple(out_specs),
        compiler_params=_cparams(("parallel", "parallel"), 14 * _nbytes((ts, D), F32)),
        name="residual_ln",
    )(*args)
    return tuple(res) + (None,) * (3 - len(res))


def _peer_scores_kernel(h_ref, wpq_ref, keys_ref, st_ref):
    q = jnp.dot(h_ref[...], wpq_ref[...], preferred_element_type=F32).astype(BF16)
    st_ref[...] = lax.dot_general(keys_ref[...], q, _NT, preferred_element_type=F32)


def _peer_scores(h, w_pq, keys_bd):
    T, D = h.shape
    HQ = w_pq.shape[1]
    R = keys_bd.shape[0]
    tm = min(512, T)
    need = 2 * (_nbytes((tm, D), BF16) + _nbytes((D, HQ), BF16) + _nbytes((R, HQ), BF16)
                + _nbytes((R, tm), F32)) + 2 * _nbytes((tm, HQ), F32)
    return pl.pallas_call(
        _peer_scores_kernel,
        out_shape=jax.ShapeDtypeStruct((R, T), F32),
        grid=(T // tm,),
        in_specs=[
            pl.BlockSpec((tm, D), lambda i: (i, 0)),
            pl.BlockSpec((D, HQ), lambda i: (0, 0)),
            pl.BlockSpec((R, HQ), lambda i: (0, 0)),
        ],
        out_specs=pl.BlockSpec((R, tm), lambda i: (0, i)),
        compiler_params=_cparams(("parallel",), need),
        name="peer_scores",
    )(h, w_pq, keys_bd)


def _top16_ranks(s, row_ids, exact):
    n = s.shape[0]
    k_ids = lax.broadcasted_iota(jnp.int32, (PEER_TOPK, s.shape[1]), 0)
    rank = jnp.full(s.shape, float(PEER_TOPK), F32)
    tops = jnp.zeros((PEER_TOPK, s.shape[1]), F32)
    for k in range(PEER_TOPK):
        m = jnp.max(s, axis=0, keepdims=True)
        sel = s == m
        if exact:
            first = jnp.min(jnp.where(sel, row_ids, float(n)), axis=0, keepdims=True)
            sel = row_ids == first
        rank = jnp.where(sel, float(k), rank)
        s = jnp.where(sel, -jnp.inf, s)
        tops = jnp.where(k_ids == k, m, tops)
    count = jnp.sum(jnp.where(rank < float(PEER_TOPK), 1.0, 0.0), axis=0, keepdims=True)
    return rank, tops, count


def _pair_counts(t1, t2, exact):
    L = t1.shape[1]
    K = PEER_TOPK
    sub = 8
    pieces = [t1[0:1] + t2]
    pos = [lax.broadcasted_iota(jnp.int32, (K, L), 0).astype(F32)]
    r8 = lax.broadcasted_iota(jnp.int32, (sub, L), 0).astype(F32)
    for a in range(1, sub):
        pieces.append(t1[a:a + 1] + t2[0:sub])
        pos.append(r8 + float(a * K))
    pieces.append(t1[sub:K] + t2[0:1])
    pos.append((r8 + float(sub)) * float(K))
    v = jnp.concatenate(pieces, axis=0)
    p = jnp.concatenate(pos, axis=0)
    picked = jnp.zeros(v.shape, F32)
    z = jnp.zeros((1, L), F32)
    top = None
    for k in range(K):
        m = jnp.max(v, axis=0, keepdims=True)
        sel = v == m
        if exact:
            first = jnp.min(jnp.where(sel, p, float(K * K)), axis=0, keepdims=True)
            sel = p == first
        picked = jnp.where(sel, 1.0, picked)
        v = jnp.where(sel, -jnp.inf, v)
        if k == 0:
            top = m
            z = z + 1.0
        else:
            z = z + jnp.exp(m - top)
    a_ids = lax.broadcasted_iota(jnp.int32, (sub, L), 0)
    low = jnp.zeros((sub, L), F32)
    low = jnp.where(a_ids == 0, jnp.sum(picked[0:K], axis=0, keepdims=True), low)
    for a in range(1, sub):
        off = K + (a - 1) * sub
        low = jnp.where(a_ids == a, jnp.sum(picked[off:off + sub], axis=0, keepdims=True), low)
    counts = jnp.concatenate([low, picked[K + (sub - 1) * sub:]], axis=0)
    return counts, z, jnp.sum(picked, axis=0, keepdims=True)


def _peer_select_kernel(st_ref, rank2_ref, cnt1_ref, e1_ref, e2_ref, *, heads, nk):
    tt = st_ref.shape[1]
    row_ids = lax.broadcasted_iota(jnp.int32, (nk, V7X_LANES), 0).astype(F32)
    full = float(PEER_TOPK)

    def per_head(idx, carry):
        c = idx // heads
        hh = idx % heads
        lanes = pl.ds(pl.multiple_of(c * V7X_LANES, V7X_LANES), V7X_LANES)
        r1 = pl.ds(pl.multiple_of((2 * hh) * nk, nk), nk)
        r2 = pl.ds(pl.multiple_of((2 * hh + 1) * nk, nk), nk)
        ro = pl.ds(pl.multiple_of(hh * nk, nk), nk)

        def select(exact):
            s1 = st_ref[r1, lanes]
            s2 = st_ref[r2, lanes]
            rank1, t1, n1 = _top16_ranks(s1, row_ids, exact)
            rank2, t2, n2 = _top16_ranks(s2, row_ids, exact)
            counts, z, n3 = _pair_counts(t1, t2, exact)
            cnt1 = jnp.zeros((nk, V7X_LANES), F32)
            for a in range(PEER_TOPK):
                cnt1 = jnp.where(rank1 == float(a), counts[a:a + 1], cnt1)
            rank2_ref[ro, lanes] = rank2.astype(rank2_ref.dtype)
            cnt1_ref[ro, lanes] = cnt1
            e1_ref[ro, lanes] = jnp.exp(s1 - t1[0:1]) / z
            e2_ref[ro, lanes] = jnp.exp(s2 - t2[0:1]).astype(e2_ref.dtype)
            return jnp.where((n1 == full) & (n2 == full) & (n3 == full), 0.0, 1.0)

        tied = jnp.max(select(exact=False)) > 0.0

        @pl.when(tied)
        def _():
            select(exact=True)

        return carry

    lax.fori_loop(0, (tt // V7X_LANES) * heads, per_head, 0)


def _peer_select(st, heads, nk):
    R, T = st.shape
    tt = min(512, T)
    blk = pl.BlockSpec((heads * nk, tt), lambda i: (0, i))
    return pl.pallas_call(
        functools.partial(_peer_select_kernel, heads=heads, nk=nk),
        out_shape=tuple(jax.ShapeDtypeStruct((heads * nk, T), dt) for dt in (BF16, F32, F32, BF16)),
        grid=(T // tt,),
        in_specs=[pl.BlockSpec((R, tt), lambda i: (0, i))],
        out_specs=(blk,) * 4,
        compiler_params=_cparams(("parallel",), 2 * _nbytes((R, tt), F32) + 8 * _nbytes((heads * nk, tt), F32)),
        name="peer_select",
    )(st)


def _peer_dense_kernel(ht_ref, wu_ref, wv_ref, rank2_ref, cnt1_ref, e1_ref, e2_ref, yt_ref,
                       a_ref, carry_ref, *, heads, nk, pieces, n_tiles):
    e = pl.program_id(1)

    @pl.when(e == 0)
    def _():
        yt_ref[...] = jnp.zeros_like(yt_ref)
        carry_ref[...] = jnp.zeros_like(carry_ref)

    keys_per_step = sum(pieces)
    tile = jnp.minimum(e, n_tiles - 1)
    ht = ht_ref[...]
    tm = ht.shape[1]
    pk = V7X_BF16_SUBLANES
    zero = jnp.zeros((), BF16)
    key0 = [sum(pieces[:p]) for p in range(len(pieces))]
    acts = [jnp.dot(wu_ref[key0[p] * nk:(key0[p] + pieces[p]) * nk, :], ht, preferred_element_type=F32)
            for p in range(len(pieces))]

    def gated(p, dst_ref, row0):
        for b in range(pieces[p]):
            i1 = tile * keys_per_step + key0[p] + b
            gate = None
            for hh in range(heads):
                row = pl.ds(hh * nk + i1, 1)
                blk = slice(hh * nk, (hh + 1) * nk)
                cnt = jnp.broadcast_to(cnt1_ref[row, :], (pk, tm)).astype(BF16)[None]
                e1 = jnp.broadcast_to(e1_ref[row, :], (pk, tm)).astype(BF16)[None]
                r2 = rank2_ref[blk, :].reshape(nk // pk, pk, tm)
                e2 = e2_ref[blk, :].reshape(nk // pk, pk, tm)
                term = jnp.where(r2 < cnt, e2 * e1, zero)
                gate = term if gate is None else gate + term
            src = slice(b * nk, (b + 1) * nk)
            dst = slice(row0 + b * nk, row0 + (b + 1) * nk)
            dst_ref[dst, :] = jax.nn.gelu(acts[p][src, :]).astype(BF16) * gate.reshape(nk, tm)

    carried = pieces[-1] * nk
    a_ref[0:carried, :] = carry_ref[...]
    for p in range(len(pieces) - 1):
        gated(p, a_ref, carried + key0[p] * nk)
    yt_ref[...] += jnp.dot(wv_ref[...], a_ref[...], preferred_element_type=F32)
    gated(len(pieces) - 1, carry_ref, 0)


def _peer_pieces(keys_per_step):
    carried = max(keys_per_step // 4, 1)
    if keys_per_step >= 4:
        return (keys_per_step - 2 * carried, carried, carried)
    return (keys_per_step - carried, carried)


def _peer_dense(ht, w_u, w_vt_padded, sel, heads, nk):
    D, T = ht.shape
    E = w_u.shape[0]
    tm = min(512, T)
    te = _peer_tile(nk)
    pieces = _peer_pieces(te // nk)
    n_tiles = E // te
    sel_spec = pl.BlockSpec((heads * nk, tm), lambda i, e: (0, i), pipeline_mode=pl.Buffered(1))
    need = _nbytes((tm, D), BF16) + 4 * _nbytes((te, D), BF16) + _nbytes((D, tm), F32) \
        + 3 * _nbytes((heads * nk, tm), F32) + 2 * _nbytes((te, tm), BF16) + 3 * _nbytes((te, tm), F32)
    return pl.pallas_call(
        functools.partial(_peer_dense_kernel, heads=heads, nk=nk, pieces=pieces, n_tiles=n_tiles),
        out_shape=jax.ShapeDtypeStruct((D, T), F32),
        grid=(T // tm, n_tiles + 1),
        in_specs=[
            pl.BlockSpec((D, tm), lambda i, e: (0, i), pipeline_mode=pl.Buffered(1)),
            pl.BlockSpec((te, D), lambda i, e: (jnp.minimum(e, n_tiles - 1), 0)),
            pl.BlockSpec((D, te), lambda i, e: (0, e)),
            sel_spec, sel_spec, sel_spec, sel_spec,
        ],
        out_specs=pl.BlockSpec((D, tm), lambda i, e: (0, i), pipeline_mode=pl.Buffered(1)),
        scratch_shapes=[pltpu.VMEM((te, tm), BF16), pltpu.VMEM((pieces[-1] * nk, tm), BF16)],
        compiler_params=_cparams(("parallel", "arbitrary"), need),
        name="peer_dense",
    )(ht, w_u, w_vt_padded, *sel)


def _peer_tile(nk):
    return min(8, nk) * nk


def _transpose_pad_kernel(w_ref, o_ref, *, n_blocks):
    j = pl.program_id(0)
    inside = (j > 0) & (j <= n_blocks)

    @pl.when(inside)
    def _():
        o_ref[...] = w_ref[...].T.astype(o_ref.dtype)

    @pl.when(jnp.logical_not(inside))
    def _():
        o_ref[...] = jnp.zeros_like(o_ref)


def _transpose_pad(w_v, front, back):
    E, D = w_v.shape
    n_blocks = E // front
    assert back % front == 0
    return pl.pallas_call(
        functools.partial(_transpose_pad_kernel, n_blocks=n_blocks),
        out_shape=jax.ShapeDtypeStruct((D, front + E + back), BF16),
        grid=(1 + n_blocks + back // front,),
        in_specs=[pl.BlockSpec((front, D), lambda j: (jnp.clip(j - 1, 0, n_blocks - 1), 0))],
        out_specs=pl.BlockSpec((D, front), lambda j: (0, j)),
        compiler_params=_cparams(("parallel",), 6 * _nbytes((front, D), F32)),
        name="transpose_pad_wv",
    )(w_v)


def _mixer_sublayer(h, w_in, layer, cos_t, sin_t, w_gate, b_gate, ln_v_g, ln_v_b, w_sp, b_sp, conv_w,
                    w_branch, w_o, seq):
    mix = ln_v_g.shape[0]
    heads = mix // HEAD_DIM
    proj_ac = _proj_ac(h, w_in, layer, mix)
    qkv = _proj_qkv(h, w_in, layer, cos_t, sin_t, mix)
    branch_a = _gmlp(proj_ac, ln_v_g, ln_v_b, w_sp, b_sp, mix)
    outs, lses = [], []
    for g, (_, dilation) in enumerate(B_PATTERNS):
        o, lse = _dilated_attention(qkv, g, dilation, seq, heads)
        outs.append(o)
        lses.append(lse)
    branch_b = _combine_groups(outs, lses)
    branch_c = _short_conv(proj_ac, conv_w, seq, mix, 2 * mix)
    merged = _merge(h, (branch_a, branch_b, branch_c), w_gate, b_gate, w_branch)
    return _matmul(merged, w_o, BF16, "out_proj")


def _peer_sublayer(h, ht, w_pq, sub_keys, w_u, w_v):
    heads, _, nk, _ = sub_keys.shape
    keys_bd = jax.scipy.linalg.block_diag(
        *[sub_keys[hh, p] for hh in range(heads) for p in range(2)]).astype(BF16)
    te = _peer_tile(nk)
    carried = _peer_pieces(te // nk)[-1] * nk
    w_vt_padded = _transpose_pad(w_v, carried, te - carried)
    st = _peer_scores(h, w_pq.astype(BF16), keys_bd)
    sel = _peer_select(st, heads, nk)
    return _peer_dense(ht, w_u.astype(BF16), w_vt_padded, sel, heads, nk)


def kernel(x, c, positions, w_ada, b_ada, w_in, w_gate, b_gate, ln_v_g, ln_v_b, w_sp, b_sp,
           conv_w, w_branch, w_o, ln1_g, ln1_b, w_pq, sub_keys, w_u, w_v, ln2_g, ln2_b):
    B, S, D = x.shape
    depth = w_ada.shape[0]
    alpha = (2 * depth) ** 0.25
    assert S % (ATTN_BLOCK * B_PATTERNS[-1][1]) == 0

    cos_t, sin_t = _rope_tables(positions)
    modr = _adaln(c, w_ada, b_ada)

    def mod_row(layer, k):
        return lambda b: (layer * B + b) * 6 + k

    h = _modulate(x, modr, mod_row(0, 1), mod_row(0, 0))
    for l in range(depth):
        y = _mixer_sublayer(h, w_in, l, cos_t, sin_t, w_gate[l].astype(BF16), b_gate[l],
                            ln_v_g[l], ln_v_b[l], w_sp[l], b_sp[l], conv_w[l],
                            w_branch[l].astype(BF16), w_o[l].astype(BF16), S)
        x, h, ht = _residual_ln(x, y, modr, mod_row(l, 2), ln1_g[l], ln1_b[l], alpha, y_transposed=False,
                                next_rows=(mod_row(l, 4), mod_row(l, 3)), emit_ht=True)
        y_t = _peer_sublayer(h, ht, w_pq[l], sub_keys[l], w_u[l], w_v[l])
        nxt = (mod_row(l + 1, 1), mod_row(l + 1, 0)) if l + 1 < depth else None
        x, h, _ = _residual_ln(x, y_t, modr, mod_row(l, 5), ln2_g[l], ln2_b[l], alpha, y_transposed=True,
                               next_rows=nxt)
    return x
```

```python
import functools

import jax
import jax.numpy as jnp
from jax import lax
from jax.experimental import pallas as pl
from jax.experimental.pallas import tpu as pltpu

F32 = jnp.float32
BF16 = jnp.bfloat16

HEAD_DIM = 128
CHUNK = 128
ATTN_BLOCK = 128
ROT_DIM = HEAD_DIM // 4
ROPE_THETA = 500000.0
B_PATTERNS = ((128, 1), (512, 4), (2048, 16))
N_BRANCH = 3
PEER_TOPK = 16
LN_EPS = 1e-5
MASKED = -1e30

V7X_LANES = 128
V7X_BF16_SUBLANES = 16
V7X_VMEM_LIMIT_CAP = 58 * 2**20
SPILL_AND_TEMP_BYTES = 4 * 2**20
ATTN_VMEM_BUDGET = 32 * 2**20

def _cparams(semantics, vmem_bytes, flags=None):
    limit = max(vmem_bytes + SPILL_AND_TEMP_BYTES, 16 * 2**20)
    return pltpu.CompilerParams(
        dimension_semantics=semantics,
        vmem_limit_bytes=int(min(limit, V7X_VMEM_LIMIT_CAP)),
        flags=flags,
    )


def _nbytes(shape, dtype):
    n = 1
    for s in shape:
        n *= s
    return n * jnp.dtype(dtype).itemsize


def _layer_norm_rows(z, g, b):
    mu = jnp.mean(z, axis=-1, keepdims=True)
    zc = z - mu
    var = jnp.mean(zc * zc, axis=-1, keepdims=True)
    return zc * lax.rsqrt(var + LN_EPS) * g + b


_NT = (((1,), (1,)), ((), ()))


def _rope_table_kernel(pos_ref, freq_ref, sign_ref, cos_ref, sin_ref):
    ang = pos_ref[...].astype(F32) * freq_ref[...]
    cos_ref[...] = jnp.cos(ang)
    sin_ref[...] = jnp.sin(ang) * sign_ref[...]


def _rope_tables(positions):
    T = positions.size
    half = ROT_DIM // 2
    inv_freq = ROPE_THETA ** (-jnp.arange(half, dtype=F32) / half)
    zeros = jnp.zeros((HEAD_DIM - ROT_DIM,), F32)
    freq = jnp.concatenate([inv_freq, inv_freq, zeros])[None, :]
    sign = jnp.concatenate([-jnp.ones((half,), F32), jnp.ones((half,), F32), zeros])[None, :]
    tm = min(1024, T)
    row = pl.BlockSpec((1, HEAD_DIM), lambda i: (0, 0))
    tab = pl.BlockSpec((tm, HEAD_DIM), lambda i: (i, 0))
    return pl.pallas_call(
        _rope_table_kernel,
        out_shape=(jax.ShapeDtypeStruct((T, HEAD_DIM), F32),) * 2,
        grid=(T // tm,),
        in_specs=[pl.BlockSpec((tm, 1), lambda i: (i, 0)), row, row],
        out_specs=(tab, tab),
        compiler_params=_cparams(("parallel",), 8 * _nbytes((tm, HEAD_DIM), F32)),
        name="rope_table",
    )(positions.reshape(T, 1), freq, sign)


def _adaln_kernel(c_ref, w_ref, b_ref, o_ref):
    c = c_ref[...]
    o_ref[...] = jnp.dot(c * jax.nn.sigmoid(c), w_ref[...], preferred_element_type=F32) + b_ref[...]


def _adaln(c, w_ada, b_ada):
    L, D, N = w_ada.shape
    B = c.shape[0]
    rows = 8
    c_pad = jnp.zeros((rows, D), F32).at[:B].set(c)
    tn = min(512, N)
    mod = pl.pallas_call(
        _adaln_kernel,
        out_shape=jax.ShapeDtypeStruct((L, rows, N), F32),
        grid=(L, N // tn),
        in_specs=[
            pl.BlockSpec((rows, D), lambda l, j: (0, 0)),
            pl.BlockSpec((None, D, tn), lambda l, j: (l, 0, j)),
            pl.BlockSpec((None, 1, tn), lambda l, j: (l, 0, j)),
        ],
        out_specs=pl.BlockSpec((None, rows, tn), lambda l, j: (l, 0, j)),
        compiler_params=_cparams(("parallel", "parallel"), 2 * _nbytes((D, tn), F32) + 2**22),
        name="adaln",
    )(c_pad, w_ada, b_ada.reshape(L, 1, N))
    return mod[:, :B].reshape(L * B * 6, 1, D)


def _modulate_kernel(x_ref, sc_ref, sh_ref, o_ref):
    o_ref[...] = (x_ref[...] * (1.0 + sc_ref[...]) + sh_ref[...]).astype(o_ref.dtype)


def _modulate(x, modr, sc_row, sh_row):
    B, S, D = x.shape
    ts = min(512, S)
    nst = S // ts
    return pl.pallas_call(
        _modulate_kernel,
        out_shape=jax.ShapeDtypeStruct((B * S, D), BF16),
        grid=(B, nst),
        in_specs=[
            pl.BlockSpec((None, ts, D), lambda b, s: (b, s, 0)),
            pl.BlockSpec((None, 1, D), lambda b, s: (sc_row(b), 0, 0)),
            pl.BlockSpec((None, 1, D), lambda b, s: (sh_row(b), 0, 0)),
        ],
        out_specs=pl.BlockSpec((ts, D), lambda b, s: (b * nst + s, 0)),
        compiler_params=_cparams(("parallel", "parallel"), 3 * _nbytes((ts, D), F32) + 2**22),
        name="modulate",
    )(x, modr, modr)


M_SPLIT = 2


def _row_chunks(tm):
    rows = tm // M_SPLIT
    return [slice(c * rows, (c + 1) * rows) for c in range(M_SPLIT)]


def _cast_weights_once(w_ref, wb_ref):
    @pl.when(pl.program_id(1) == 0)
    def _():
        wb_ref[...] = w_ref[...].astype(wb_ref.dtype)


def _proj_ac_kernel(a_ref, w_ref, o_ref, wb_ref, *, n_gelu):
    j = pl.program_id(0)
    _cast_weights_once(w_ref, wb_ref)

    @pl.when(j < n_gelu)
    def _():
        for rs in _row_chunks(a_ref.shape[0]):
            acc = jnp.dot(a_ref[rs, :], wb_ref[...], preferred_element_type=F32)
            o_ref[rs, :] = jax.nn.gelu(acc).astype(o_ref.dtype)

    @pl.when(j >= n_gelu)
    def _():
        for rs in _row_chunks(a_ref.shape[0]):
            acc = jnp.dot(a_ref[rs, :], wb_ref[...], preferred_element_type=F32)
            o_ref[rs, :] = acc.astype(o_ref.dtype)


def _proj_vmem(tm, D, tn, out_dtype):
    return 2 * (_nbytes((tm, D), BF16) + _nbytes((D, tn), F32) + _nbytes((tm, tn), out_dtype)) \
        + _nbytes((D, tn), BF16) + 4 * _nbytes((tm, tn), F32)


def _proj_ac(h, w_in, layer, mix):
    T, D = h.shape
    tm = min(1024, T)
    tn = min(512, mix)
    n_gelu = 2 * mix // tn
    skip = 9 * mix // tn

    def w_col(j, i):
        return (layer, 0, jnp.where(j < n_gelu, j, j + skip))

    return pl.pallas_call(
        functools.partial(_proj_ac_kernel, n_gelu=n_gelu),
        out_shape=jax.ShapeDtypeStruct((T, 5 * mix), BF16),
        grid=(5 * mix // tn, T // tm),
        in_specs=[pl.BlockSpec((tm, D), lambda j, i: (i, 0)), pl.BlockSpec((None, D, tn), w_col)],
        out_specs=pl.BlockSpec((tm, tn), lambda j, i: (i, j)),
        scratch_shapes=[pltpu.VMEM((D, tn), BF16)],
        compiler_params=_cparams(("arbitrary", "arbitrary"), _proj_vmem(tm, D, tn, BF16)),
        name="proj_ac",
    )(h, w_in)


def _store_heads(o_ref, rs, val):
    for hh in range(val.shape[1] // HEAD_DIM):
        o_ref[hh, rs, :] = val[:, hh * HEAD_DIM:(hh + 1) * HEAD_DIM]


def _proj_qkv_kernel(a_ref, w_ref, cos_ref, sin_ref, o_ref, wb_ref, *, n_rope):
    j = pl.program_id(0)
    tn = wb_ref.shape[1]
    _cast_weights_once(w_ref, wb_ref)

    @pl.when(j < n_rope)
    def _():
        reps = tn // HEAD_DIM
        half = ROT_DIM // 2
        for rs in _row_chunks(a_ref.shape[0]):
            acc = jnp.dot(a_ref[rs, :], wb_ref[...], preferred_element_type=F32)
            cos = jnp.concatenate([cos_ref[rs, :]] * reps, axis=1)
            sin = jnp.concatenate([sin_ref[rs, :]] * reps, axis=1)
            lane = lax.broadcasted_iota(jnp.int32, acc.shape, 1) % HEAD_DIM
            partner = jnp.where(lane < half, pltpu.roll(acc, tn - half, 1), pltpu.roll(acc, half, 1))
            _store_heads(o_ref, rs, acc * cos + partner * sin)

    @pl.when(j >= n_rope)
    def _():
        for rs in _row_chunks(a_ref.shape[0]):
            _store_heads(o_ref, rs, jnp.dot(a_ref[rs, :], wb_ref[...], preferred_element_type=F32))


def _proj_qkv(h, w_in, layer, cos_t, sin_t, mix):
    T, D = h.shape
    tm = min(1024, T)
    tn = min(512, mix)
    hpt = tn // HEAD_DIM
    first = 2 * mix // tn
    tab = pl.BlockSpec((tm, HEAD_DIM), lambda j, i: (i, 0))
    return pl.pallas_call(
        functools.partial(_proj_qkv_kernel, n_rope=6 * mix // tn),
        out_shape=jax.ShapeDtypeStruct((9 * mix // HEAD_DIM, T, HEAD_DIM), F32),
        grid=(9 * mix // tn, T // tm),
        in_specs=[
            pl.BlockSpec((tm, D), lambda j, i: (i, 0)),
            pl.BlockSpec((None, D, tn), lambda j, i: (layer, 0, first + j)),
            tab, tab,
        ],
        out_specs=pl.BlockSpec((hpt, tm, HEAD_DIM), lambda j, i: (j, i, 0)),
        scratch_shapes=[pltpu.VMEM((D, tn), BF16)],
        compiler_params=_cparams(("arbitrary", "arbitrary"),
                                 _proj_vmem(tm, D, tn, F32) + 4 * _nbytes((tm, HEAD_DIM), F32)),
        name="proj_qkv",
    )(h, w_in, cos_t, sin_t)


def _gmlp_kernel(u_ref, v_ref, g_ref, b_ref, wsp_ref, bspt_ref, o_ref, *, groups):
    tg = u_ref.shape[0]
    vn = _layer_norm_rows(v_ref[...].astype(F32), g_ref[...], b_ref[...]).astype(BF16)
    t_idx = lax.broadcasted_iota(jnp.int32, (CHUNK, CHUNK), 0)
    s_idx = lax.broadcasted_iota(jnp.int32, (CHUNK, CHUNK), 1)
    causal = s_idx <= t_idx
    for g in range(groups):
        cols = slice(g * CHUNK, (g + 1) * CHUNK)
        w = jnp.where(causal, wsp_ref[g], 0.0).astype(BF16)
        bias = bspt_ref[:, g:g + 1]
        for c in range(tg // CHUNK):
            rows = slice(c * CHUNK, (c + 1) * CHUNK)
            mixed = jnp.dot(w, vn[rows, cols], preferred_element_type=F32) + bias
            o_ref[rows, cols] = (u_ref[rows, cols].astype(F32) * mixed).astype(o_ref.dtype)


def _gmlp(proj, ln_g, ln_b, w_sp, b_sp, mix):
    T = proj.shape[0]
    groups = w_sp.shape[0]
    tg = min(512, T)
    row = pl.BlockSpec((1, mix), lambda i: (0, 0))
    return pl.pallas_call(
        functools.partial(_gmlp_kernel, groups=groups),
        out_shape=jax.ShapeDtypeStruct((T, mix), BF16),
        grid=(T // tg,),
        in_specs=[
            pl.BlockSpec((tg, mix), lambda i: (i, 0)),
            pl.BlockSpec((tg, mix), lambda i: (i, 1)),
            row, row,
            pl.BlockSpec((groups, CHUNK, CHUNK), lambda i: (0, 0, 0)),
            pl.BlockSpec((CHUNK, groups), lambda i: (0, 0)),
        ],
        out_specs=pl.BlockSpec((tg, mix), lambda i: (i, 0)),
        compiler_params=_cparams(("parallel",), 10 * _nbytes((tg, mix), F32)),
        name="gmlp",
    )(proj, proj, ln_g.reshape(1, mix), ln_b.reshape(1, mix), w_sp, b_sp.T)


def _attn_kernel(q_ref, kc_ref, kp_ref, vc_ref, vp_ref, o_ref, lse_ref, *, dilation, periods_per_seq):
    has_prev = (pl.program_id(0) % periods_per_seq) > 0
    qi = lax.broadcasted_iota(jnp.int32, (ATTN_BLOCK, ATTN_BLOCK), 0)
    kj = lax.broadcasted_iota(jnp.int32, (ATTN_BLOCK, ATTN_BLOCK), 1)
    mask_c = kj <= qi
    mask_p = (kj >= qi) & has_prev
    scale = HEAD_DIM ** -0.5

    def one_class(r, carry):
        rows = pl.ds(r, ATTN_BLOCK, stride=dilation) if dilation > 1 else pl.ds(0, ATTN_BLOCK)
        for h in range(q_ref.shape[0]):
            q = q_ref[h, rows, :].astype(BF16)
            s_c = lax.dot_general(q, kc_ref[h, rows, :].astype(BF16), _NT, preferred_element_type=F32) * scale
            s_p = lax.dot_general(q, kp_ref[h, rows, :].astype(BF16), _NT, preferred_element_type=F32) * scale
            s_c = jnp.where(mask_c, s_c, MASKED)
            s_p = jnp.where(mask_p, s_p, MASKED)
            m = jnp.maximum(jnp.max(s_c, axis=-1, keepdims=True), jnp.max(s_p, axis=-1, keepdims=True))
            p_c = jnp.exp(s_c - m)
            p_p = jnp.exp(s_p - m)
            l = jnp.sum(p_c, axis=-1, keepdims=True) + jnp.sum(p_p, axis=-1, keepdims=True)
            o = jnp.dot(p_c.astype(BF16), vc_ref[h, rows, :].astype(BF16), preferred_element_type=F32)
            o = o + jnp.dot(p_p.astype(BF16), vp_ref[h, rows, :].astype(BF16), preferred_element_type=F32)
            o_ref[h, rows, :] = o / l
            lse_ref[h, rows, :] = jnp.broadcast_to(m + jnp.log(l), (ATTN_BLOCK, HEAD_DIM))
        return carry

    if dilation > 1:
        lax.fori_loop(0, dilation, one_class, 0)
    else:
        one_class(0, 0)


def _dilated_attention(qkv, group, dilation, seq, heads):
    T = qkv.shape[1]
    period = ATTN_BLOCK * dilation
    periods_per_seq = seq // period
    hps = heads
    while 14 * _nbytes((hps, period, HEAD_DIM), F32) > ATTN_VMEM_BUDGET and hps % 2 == 0:
        hps //= 2

    def cur(part):
        return pl.BlockSpec((hps, period, HEAD_DIM), lambda p, h: ((3 * part + group) * (heads // hps) + h, p, 0))

    def prev(part):
        return pl.BlockSpec((hps, period, HEAD_DIM),
                            lambda p, h: ((3 * part + group) * (heads // hps) + h, jnp.maximum(p - 1, 0), 0))

    out = pl.BlockSpec((hps, period, HEAD_DIM), lambda p, h: (h, p, 0))
    shape = jax.ShapeDtypeStruct((heads, T, HEAD_DIM), F32)
    return pl.pallas_call(
        functools.partial(_attn_kernel, dilation=dilation, periods_per_seq=periods_per_seq),
        out_shape=(shape, shape),
        grid=(T // period, heads // hps),
        in_specs=[cur(0), cur(1), prev(1), cur(2), prev(2)],
        out_specs=(out, out),
        compiler_params=_cparams(("parallel", "parallel"), 14 * _nbytes((hps, period, HEAD_DIM), F32)),
        name=f"dilated_attention_d{dilation}",
    )(qkv, qkv, qkv, qkv, qkv)


def _combine_kernel(o0, o1, o2, l0, l1, l2, out_ref):
    for h in range(o0.shape[0]):
        a, b, c = l0[h], l1[h], l2[h]
        m = jnp.maximum(jnp.maximum(a, b), c)
        ea, eb, ec = jnp.exp(a - m), jnp.exp(b - m), jnp.exp(c - m)
        num = ea * o0[h] + eb * o1[h] + ec * o2[h]
        out_ref[:, h * HEAD_DIM:(h + 1) * HEAD_DIM] = (num / (ea + eb + ec)).astype(out_ref.dtype)


def _combine_groups(outs, lses):
    heads, T, _ = outs[0].shape
    tm = min(512, T)
    blk = pl.BlockSpec((heads, tm, HEAD_DIM), lambda i: (0, i, 0))
    return pl.pallas_call(
        _combine_kernel,
        out_shape=jax.ShapeDtypeStruct((T, heads * HEAD_DIM), BF16),
        grid=(T // tm,),
        in_specs=[blk] * 6,
        out_specs=pl.BlockSpec((tm, heads * HEAD_DIM), lambda i: (i, 0)),
        compiler_params=_cparams(("parallel",), 14 * _nbytes((heads, tm, HEAD_DIM), F32)),
        name="attn_combine",
    )(*outs, *lses)


def _conv_kernel(gb_ref, gc_ref, xin_ref, gcp_ref, xinp_ref, cw_ref, o_ref, *, tiles_per_seq):
    first = (pl.program_id(0) % tiles_per_seq) == 0
    z = gc_ref[...].astype(F32) * xin_ref[...].astype(F32)
    zp = gcp_ref[...].astype(F32) * xinp_ref[...].astype(F32)
    zp = jnp.where(first, 0.0, zp)
    last = zp.shape[0] - 1
    rows = lax.broadcasted_iota(jnp.int32, z.shape, 0)
    z1 = jnp.where(rows == 0, zp[last:last + 1], pltpu.roll(z, 1, 0))
    z2 = jnp.where(rows == 0, zp[last - 1:last], jnp.where(rows == 1, zp[last:last + 1], pltpu.roll(z, 2, 0)))
    y = cw_ref[0:1, :] * z2 + cw_ref[1:2, :] * z1 + cw_ref[2:3, :] * z
    o_ref[...] = (gb_ref[...].astype(F32) * y).astype(o_ref.dtype)


def _short_conv(proj, conv_w, seq, mix, col0):
    T = proj.shape[0]
    tc = min(512, seq)
    tiles_per_seq = seq // tc
    halo = V7X_BF16_SUBLANES
    cb = col0 // mix
    per_halo = tc // halo

    def cur(k):
        return pl.BlockSpec((tc, mix), lambda i: (i, cb + k))

    def prev(k):
        return pl.BlockSpec((halo, mix), lambda i: (jnp.maximum(i * per_halo - 1, 0), cb + k))

    return pl.pallas_call(
        functools.partial(_conv_kernel, tiles_per_seq=tiles_per_seq),
        out_shape=jax.ShapeDtypeStruct((T, mix), BF16),
        grid=(T // tc,),
        in_specs=[cur(0), cur(1), cur(2), prev(1), prev(2),
                  pl.BlockSpec(conv_w.shape, lambda i: (0, 0))],
        out_specs=pl.BlockSpec((tc, mix), lambda i: (i, 0)),
        compiler_params=_cparams(("parallel",), 16 * _nbytes((tc, mix), F32)),
        name="short_conv",
    )(proj, proj, proj, proj, proj, conv_w)


def _merge_kernel(h_ref, ba_ref, bb_ref, bc_ref, wg_ref, bg_ref, wb_ref, o_ref, wgb_ref):
    _cast_weights_once(wg_ref, wgb_ref)
    h = h_ref[...]
    total = None
    for g, br in enumerate((ba_ref, bb_ref, bc_ref)):
        gate = jax.nn.sigmoid(jnp.dot(h, wgb_ref[g], preferred_element_type=F32) + bg_ref[g])
        term = gate * jnp.dot(br[...], wb_ref[g], preferred_element_type=F32)
        total = term if total is None else total + term
    o_ref[...] = total.astype(o_ref.dtype)


def _merge(h, branches, w_gate, layer, b_gate, w_branch):
    T, D = h.shape
    mix = branches[0].shape[1]
    tm = min(512, T)
    tn = min(256, D)
    br = pl.BlockSpec((tm, mix), lambda j, i: (i, 0))
    need = 2 * (_nbytes((tm, D), BF16) + 3 * _nbytes((tm, mix), BF16) + 3 * _nbytes((D, tn), F32)
                + 3 * _nbytes((mix, tn), BF16) + _nbytes((tm, tn), BF16)) \
        + 3 * _nbytes((D, tn), BF16) + 8 * _nbytes((tm, tn), F32)
    return pl.pallas_call(
        _merge_kernel,
        out_shape=jax.ShapeDtypeStruct((T, D), BF16),
        grid=(D // tn, T // tm),
        in_specs=[
            pl.BlockSpec((tm, D), lambda j, i: (i, 0)),
            br, br, br,
            pl.BlockSpec((None, N_BRANCH, D, tn), lambda j, i: (layer, 0, 0, j)),
            pl.BlockSpec((N_BRANCH, 1, tn), lambda j, i: (0, 0, j)),
            pl.BlockSpec((N_BRANCH, mix, tn), lambda j, i: (0, 0, j)),
        ],
        out_specs=pl.BlockSpec((tm, tn), lambda j, i: (i, j)),
        scratch_shapes=[pltpu.VMEM((N_BRANCH, D, tn), BF16)],
        compiler_params=_cparams(("arbitrary", "arbitrary"), need),
        name="branch_merge",
    )(h, *branches, w_gate, b_gate.reshape(N_BRANCH, 1, D), w_branch)


def _matmul_kernel(a_ref, w_ref, o_ref, wb_ref):
    _cast_weights_once(w_ref, wb_ref)
    o_ref[...] = jnp.dot(a_ref[...], wb_ref[...], preferred_element_type=F32).astype(o_ref.dtype)


def _matmul(a, w_stack, layer, out_dtype, name):
    T, K = a.shape
    N = w_stack.shape[2]
    tm = min(1024, T)
    tn = min(512, N)
    return pl.pallas_call(
        _matmul_kernel,
        out_shape=jax.ShapeDtypeStruct((T, N), out_dtype),
        grid=(N // tn, T // tm),
        in_specs=[pl.BlockSpec((tm, K), lambda j, i: (i, 0)),
                  pl.BlockSpec((None, K, tn), lambda j, i: (layer, 0, j))],
        out_specs=pl.BlockSpec((tm, tn), lambda j, i: (i, j)),
        scratch_shapes=[pltpu.VMEM((K, tn), BF16)],
        compiler_params=_cparams(("arbitrary", "arbitrary"), _proj_vmem(tm, K, tn, out_dtype)),
        name=name,
    )(a, w_stack)


def _residual_ln_kernel(x_ref, y_ref, gate_ref, lg_ref, lb_ref, *rest, alpha, y_transposed, emit_h, emit_ht):
    y = y_ref[...].astype(F32)
    if y_transposed:
        y = y.T
    xn = _layer_norm_rows(alpha * x_ref[...] + gate_ref[...] * y, lg_ref[...], lb_ref[...])
    if emit_h:
        sc_ref, sh_ref, xo_ref, ho_ref = rest[:4]
        h = xn * (1.0 + sc_ref[...]) + sh_ref[...]
        ho_ref[...] = h.astype(ho_ref.dtype)
        if emit_ht:
            rest[4][...] = h.T.astype(ho_ref.dtype)
    else:
        (xo_ref,) = rest
    xo_ref[...] = xn


def _residual_ln(x, y, modr, gate_row, ln_g, ln_b, alpha, *, y_transposed, next_rows=None, emit_ht=False):
    B, S, D = x.shape
    ts = min(256, S)
    nst = S // ts
    emit_h = next_rows is not None

    def mod_spec(row_fn):
        return pl.BlockSpec((None, 1, D), lambda b, s: (row_fn(b), 0, 0))

    vec = pl.BlockSpec((1, D), lambda b, s: (0, 0))
    if y_transposed:
        y_spec = pl.BlockSpec((D, ts), lambda b, s: (0, b * nst + s))
    else:
        y_spec = pl.BlockSpec((ts, D), lambda b, s: (b * nst + s, 0))
    x_spec = pl.BlockSpec((None, ts, D), lambda b, s: (b, s, 0))
    in_specs = [x_spec, y_spec, mod_spec(gate_row), vec, vec]
    args = [x, y, modr, ln_g.reshape(1, D), ln_b.reshape(1, D)]
    out_shape = [jax.ShapeDtypeStruct((B, S, D), F32)]
    out_specs = [x_spec]
    if emit_h:
        in_specs += [mod_spec(next_rows[0]), mod_spec(next_rows[1])]
        args += [modr, modr]
        out_shape.append(jax.ShapeDtypeStruct((B * S, D), BF16))
        out_specs.append(pl.BlockSpec((ts, D), lambda b, s: (b * nst + s, 0)))
        if emit_ht:
            out_shape.append(jax.ShapeDtypeStruct((D, B * S), BF16))
            out_specs.append(pl.BlockSpec((D, ts), lambda b, s: (0, b * nst + s)))
    res = pl.pallas_call(
        functools.partial(_residual_ln_kernel, alpha=alpha, y_transposed=y_transposed, emit_h=emit_h,
                          emit_ht=emit_ht),
        out_shape=tuple(out_shape),
        grid=(B, nst),
        in_specs=in_specs,
        out_specs=tuple(out_specs),
        compiler_params=_cparams(("parallel", "parallel"), 14 * _nbytes((ts, D), F32)),
        name="residual_ln",
    )(*args)
    return tuple(res) + (None,) * (3 - len(res))


def _peer_scores_kernel(h_ref, wpq_ref, keys_ref, st_ref):
    q = jnp.dot(h_ref[...], wpq_ref[...], preferred_element_type=F32).astype(BF16)
    st_ref[...] = lax.dot_general(keys_ref[...], q, _NT, preferred_element_type=F32)


def _peer_scores(h, w_pq, keys_bd):
    T, D = h.shape
    HQ = w_pq.shape[1]
    R = keys_bd.shape[0]
    tm = min(512, T)
    need = 2 * (_nbytes((tm, D), BF16) + _nbytes((D, HQ), BF16) + _nbytes((R, HQ), BF16)
                + _nbytes((R, tm), F32)) + 2 * _nbytes((tm, HQ), F32)
    return pl.pallas_call(
        _peer_scores_kernel,
        out_shape=jax.ShapeDtypeStruct((R, T), F32),
        grid=(T // tm,),
        in_specs=[
            pl.BlockSpec((tm, D), lambda i: (i, 0)),
            pl.BlockSpec((D, HQ), lambda i: (0, 0)),
            pl.BlockSpec((R, HQ), lambda i: (0, 0)),
        ],
        out_specs=pl.BlockSpec((R, tm), lambda i: (0, i)),
        compiler_params=_cparams(("parallel",), need),
        name="peer_scores",
    )(h, w_pq, keys_bd)


def _top16_ranks(s, row_ids, exact):
    n = s.shape[0]
    k_ids = lax.broadcasted_iota(jnp.int32, (PEER_TOPK, s.shape[1]), 0)
    rank = jnp.full(s.shape, float(PEER_TOPK), F32)
    tops = jnp.zeros((PEER_TOPK, s.shape[1]), F32)
    for k in range(PEER_TOPK):
        m = jnp.max(s, axis=0, keepdims=True)
        sel = s == m
        if exact:
            first = jnp.min(jnp.where(sel, row_ids, float(n)), axis=0, keepdims=True)
            sel = row_ids == first
        rank = jnp.where(sel, float(k), rank)
        s = jnp.where(sel, -jnp.inf, s)
        tops = jnp.where(k_ids == k, m, tops)
    count = jnp.sum(jnp.where(rank < float(PEER_TOPK), 1.0, 0.0), axis=0, keepdims=True)
    return rank, tops, count


def _pair_counts(t1, t2, exact):
    L = t1.shape[1]
    K = PEER_TOPK
    sub = 8
    pieces = [t1[0:1] + t2]
    pos = [lax.broadcasted_iota(jnp.int32, (K, L), 0).astype(F32)]
    r8 = lax.broadcasted_iota(jnp.int32, (sub, L), 0).astype(F32)
    for a in range(1, sub):
        pieces.append(t1[a:a + 1] + t2[0:sub])
        pos.append(r8 + float(a * K))
    pieces.append(t1[sub:K] + t2[0:1])
    pos.append((r8 + float(sub)) * float(K))
    v = jnp.concatenate(pieces, axis=0)
    p = jnp.concatenate(pos, axis=0)
    picked = jnp.zeros(v.shape, F32)
    z = jnp.zeros((1, L), F32)
    top = None
    for k in range(K):
        m = jnp.max(v, axis=0, keepdims=True)
        sel = v == m
        if exact:
            first = jnp.min(jnp.where(sel, p, float(K * K)), axis=0, keepdims=True)
            sel = p == first
        picked = jnp.where(sel, 1.0, picked)
        v = jnp.where(sel, -jnp.inf, v)
        if k == 0:
            top = m
            z = z + 1.0
        else:
            z = z + jnp.exp(m - top)
    a_ids = lax.broadcasted_iota(jnp.int32, (sub, L), 0)
    low = jnp.zeros((sub, L), F32)
    low = jnp.where(a_ids == 0, jnp.sum(picked[0:K], axis=0, keepdims=True), low)
    for a in range(1, sub):
        off = K + (a - 1) * sub
        low = jnp.where(a_ids == a, jnp.sum(picked[off:off + sub], axis=0, keepdims=True), low)
    counts = jnp.concatenate([low, picked[K + (sub - 1) * sub:]], axis=0)
    return counts, z, jnp.sum(picked, axis=0, keepdims=True)


def _peer_select_kernel(st_ref, rank2_ref, cnt1_ref, e1_ref, e2_ref, *, heads, nk):
    tt = st_ref.shape[1]
    row_ids = lax.broadcasted_iota(jnp.int32, (nk, V7X_LANES), 0).astype(F32)
    full = float(PEER_TOPK)

    def per_head(idx, carry):
        c = idx // heads
        hh = idx % heads
        lanes = pl.ds(pl.multiple_of(c * V7X_LANES, V7X_LANES), V7X_LANES)
        r1 = pl.ds(pl.multiple_of((2 * hh) * nk, nk), nk)
        r2 = pl.ds(pl.multiple_of((2 * hh + 1) * nk, nk), nk)
        ro = pl.ds(pl.multiple_of(hh * nk, nk), nk)

        def select(exact):
            s1 = st_ref[r1, lanes]
            s2 = st_ref[r2, lanes]
            rank1, t1, n1 = _top16_ranks(s1, row_ids, exact)
            rank2, t2, n2 = _top16_ranks(s2, row_ids, exact)
            counts, z, n3 = _pair_counts(t1, t2, exact)
            cnt1 = jnp.zeros((nk, V7X_LANES), F32)
            for a in range(PEER_TOPK):
                cnt1 = jnp.where(rank1 == float(a), counts[a:a + 1], cnt1)
            rank2_ref[ro, lanes] = rank2.astype(rank2_ref.dtype)
            cnt1_ref[ro, lanes] = cnt1
            e1_ref[ro, lanes] = jnp.exp(s1 - t1[0:1]) / z
            e2_ref[ro, lanes] = jnp.exp(s2 - t2[0:1]).astype(e2_ref.dtype)
            return jnp.where((n1 == full) & (n2 == full) & (n3 == full), 0.0, 1.0)

        tied = jnp.max(select(exact=False)) > 0.0

        @pl.when(tied)
        def _():
            select(exact=True)

        return carry

    lax.fori_loop(0, (tt // V7X_LANES) * heads, per_head, 0)


def _peer_select(st, heads, nk):
    R, T = st.shape
    tt = min(512, T)
    blk = pl.BlockSpec((heads * nk, tt), lambda i: (0, i))
    return pl.pallas_call(
        functools.partial(_peer_select_kernel, heads=heads, nk=nk),
        out_shape=tuple(jax.ShapeDtypeStruct((heads * nk, T), dt) for dt in (BF16, F32, F32, BF16)),
        grid=(T // tt,),
        in_specs=[pl.BlockSpec((R, tt), lambda i: (0, i))],
        out_specs=(blk,) * 4,
        compiler_params=_cparams(("parallel",), 2 * _nbytes((R, tt), F32) + 8 * _nbytes((heads * nk, tt), F32)),
        name="peer_select",
    )(st)


def _peer_dense_kernel(ht_ref, wu_ref, wv_ref, rank2_ref, cnt1_ref, e1_ref, e2_ref, yt_ref,
                       a_ref, carry_ref, *, heads, nk, pieces, n_tiles):
    e = pl.program_id(1)

    @pl.when(e == 0)
    def _():
        yt_ref[...] = jnp.zeros_like(yt_ref)
        carry_ref[...] = jnp.zeros_like(carry_ref)

    keys_per_step = sum(pieces)
    tile = jnp.minimum(e, n_tiles - 1)
    ht = ht_ref[...]
    tm = ht.shape[1]
    pk = V7X_BF16_SUBLANES
    zero = jnp.zeros((), BF16)
    key0 = [sum(pieces[:p]) for p in range(len(pieces))]
    acts = [jnp.dot(wu_ref[key0[p] * nk:(key0[p] + pieces[p]) * nk, :], ht, preferred_element_type=F32)
            for p in range(len(pieces))]

    def gated(p, dst_ref, row0):
        for b in range(pieces[p]):
            i1 = tile * keys_per_step + key0[p] + b
            gate = None
            for hh in range(heads):
                row = pl.ds(hh * nk + i1, 1)
                blk = slice(hh * nk, (hh + 1) * nk)
                cnt = jnp.broadcast_to(cnt1_ref[row, :], (pk, tm)).astype(BF16)[None]
                e1 = jnp.broadcast_to(e1_ref[row, :], (pk, tm)).astype(BF16)[None]
                r2 = rank2_ref[blk, :].reshape(nk // pk, pk, tm)
                e2 = e2_ref[blk, :].reshape(nk // pk, pk, tm)
                term = jnp.where(r2 < cnt, e2 * e1, zero)
                gate = term if gate is None else gate + term
            src = slice(b * nk, (b + 1) * nk)
            dst = slice(row0 + b * nk, row0 + (b + 1) * nk)
            dst_ref[dst, :] = jax.nn.gelu(acts[p][src, :]).astype(BF16) * gate.reshape(nk, tm)

    carried = pieces[-1] * nk
    a_ref[0:carried, :] = carry_ref[...]
    for p in range(len(pieces) - 1):
        gated(p, a_ref, carried + key0[p] * nk)
    yt_ref[...] += jnp.dot(wv_ref[...], a_ref[...], preferred_element_type=F32)
    gated(len(pieces) - 1, carry_ref, 0)


def _peer_pieces(keys_per_step):
    if keys_per_step >= 4:
        return (keys_per_step - 2, 1, 1)
    return (keys_per_step - 1, 1)


def _peer_dense(ht, w_u, w_vt_padded, sel, heads, nk):
    D, T = ht.shape
    E = w_u.shape[0]
    tm = min(512, T)
    te = _peer_tile(nk)
    pieces = _peer_pieces(te // nk)
    n_tiles = E // te
    sel_spec = pl.BlockSpec((heads * nk, tm), lambda i, e: (0, i), pipeline_mode=pl.Buffered(1))
    need = _nbytes((tm, D), BF16) + 4 * _nbytes((te, D), BF16) + 2 * _nbytes((D, tm), F32) \
        + 4 * _nbytes((heads * nk, tm), F32) + 2 * _nbytes((te, tm), BF16) + 4 * _nbytes((te, tm), F32)
    return pl.pallas_call(
        functools.partial(_peer_dense_kernel, heads=heads, nk=nk, pieces=pieces, n_tiles=n_tiles),
        out_shape=jax.ShapeDtypeStruct((D, T), F32),
        grid=(T // tm, n_tiles + 1),
        in_specs=[
            pl.BlockSpec((D, tm), lambda i, e: (0, i), pipeline_mode=pl.Buffered(1)),
            pl.BlockSpec((te, D), lambda i, e: (jnp.minimum(e, n_tiles - 1), 0)),
            pl.BlockSpec((D, te), lambda i, e: (0, e)),
            sel_spec, sel_spec, sel_spec, sel_spec,
        ],
        out_specs=pl.BlockSpec((D, tm), lambda i, e: (0, i)),
        scratch_shapes=[pltpu.VMEM((te, tm), BF16), pltpu.VMEM((pieces[-1] * nk, tm), BF16)],
        compiler_params=_cparams(("parallel", "arbitrary"), need),
        name="peer_dense",
    )(ht, w_u, w_vt_padded, *sel)


def _peer_tile(nk):
    return min(4, nk) * nk


def _transpose_pad_kernel(w_ref, o_ref, *, n_blocks):
    j = pl.program_id(0)
    inside = (j > 0) & (j <= n_blocks)

    @pl.when(inside)
    def _():
        o_ref[...] = w_ref[...].T.astype(o_ref.dtype)

    @pl.when(jnp.logical_not(inside))
    def _():
        o_ref[...] = jnp.zeros_like(o_ref)


def _transpose_pad(w_v, front, back):
    E, D = w_v.shape
    n_blocks = E // front
    assert back % front == 0
    return pl.pallas_call(
        functools.partial(_transpose_pad_kernel, n_blocks=n_blocks),
        out_shape=jax.ShapeDtypeStruct((D, front + E + back), BF16),
        grid=(1 + n_blocks + back // front,),
        in_specs=[pl.BlockSpec((front, D), lambda j: (jnp.clip(j - 1, 0, n_blocks - 1), 0))],
        out_specs=pl.BlockSpec((D, front), lambda j: (0, j)),
        compiler_params=_cparams(("parallel",), 6 * _nbytes((front, D), F32)),
        name="transpose_pad_wv",
    )(w_v)


def _mixer_sublayer(h, w_in, layer, cos_t, sin_t, w_gate, b_gate, ln_v_g, ln_v_b, w_sp, b_sp, conv_w,
                    w_branch, w_o, seq):
    mix = ln_v_g.shape[0]
    heads = mix // HEAD_DIM
    proj_ac = _proj_ac(h, w_in, layer, mix)
    qkv = _proj_qkv(h, w_in, layer, cos_t, sin_t, mix)
    branch_a = _gmlp(proj_ac, ln_v_g, ln_v_b, w_sp, b_sp, mix)
    outs, lses = [], []
    for g, (_, dilation) in enumerate(B_PATTERNS):
        o, lse = _dilated_attention(qkv, g, dilation, seq, heads)
        outs.append(o)
        lses.append(lse)
    branch_b = _combine_groups(outs, lses)
    branch_c = _short_conv(proj_ac, conv_w, seq, mix, 2 * mix)
    merged = _merge(h, (branch_a, branch_b, branch_c), w_gate, layer, b_gate, w_branch)
    return _matmul(merged, w_o, layer, BF16, "out_proj")


def _peer_sublayer(h, ht, w_pq, sub_keys, w_u, w_v):
    heads, _, nk, _ = sub_keys.shape
    keys_bd = jax.scipy.linalg.block_diag(
        *[sub_keys[hh, p] for hh in range(heads) for p in range(2)]).astype(BF16)
    te = _peer_tile(nk)
    carried = _peer_pieces(te // nk)[-1] * nk
    w_vt_padded = _transpose_pad(w_v, carried, te - carried)
    st = _peer_scores(h, w_pq.astype(BF16), keys_bd)
    sel = _peer_select(st, heads, nk)
    return _peer_dense(ht, w_u.astype(BF16), w_vt_padded, sel, heads, nk)


def kernel(x, c, positions, w_ada, b_ada, w_in, w_gate, b_gate, ln_v_g, ln_v_b, w_sp, b_sp,
           conv_w, w_branch, w_o, ln1_g, ln1_b, w_pq, sub_keys, w_u, w_v, ln2_g, ln2_b):
    B, S, D = x.shape
    depth = w_ada.shape[0]
    alpha = (2 * depth) ** 0.25
    assert S % (ATTN_BLOCK * B_PATTERNS[-1][1]) == 0

    cos_t, sin_t = _rope_tables(positions)
    modr = _adaln(c, w_ada, b_ada)

    def mod_row(layer, k):
        return lambda b: (layer * B + b) * 6 + k

    h = _modulate(x, modr, mod_row(0, 1), mod_row(0, 0))
    for l in range(depth):
        y = _mixer_sublayer(h, w_in, l, cos_t, sin_t, w_gate, b_gate[l],
                            ln_v_g[l], ln_v_b[l], w_sp[l], b_sp[l], conv_w[l],
                            w_branch[l].astype(BF16), w_o, S)
        x, h, ht = _residual_ln(x, y, modr, mod_row(l, 2), ln1_g[l], ln1_b[l], alpha, y_transposed=False,
                                next_rows=(mod_row(l, 4), mod_row(l, 3)), emit_ht=True)
        y_t = _peer_sublayer(h, ht, w_pq[l], sub_keys[l], w_u[l], w_v[l])
        nxt = (mod_row(l + 1, 1), mod_row(l + 1, 0)) if l + 1 < depth else None
        x, h, _ = _residual_ln(x, y_t, modr, mod_row(l, 5), ln2_g[l], ln2_b[l], alpha, y_transposed=True,
                               next_rows=nxt)
    return x
```

```python
import functools

import jax
import jax.numpy as jnp
from jax import lax
from jax.experimental import pallas as pl
from jax.experimental.pallas import tpu as pltpu

F32 = jnp.float32
BF16 = jnp.bfloat16

HEAD_DIM = 128
CHUNK = 128
ATTN_BLOCK = 128
ROT_DIM = HEAD_DIM // 4
ROPE_THETA = 500000.0
B_PATTERNS = ((128, 1), (512, 4), (2048, 16))
N_BRANCH = 3
PEER_TOPK = 16
LN_EPS = 1e-5
MASKED = -1e30

V7X_LANES = 128
V7X_BF16_SUBLANES = 16
V7X_VMEM_LIMIT_CAP = 58 * 2**20
SPILL_AND_TEMP_BYTES = 4 * 2**20
ATTN_VMEM_BUDGET = 32 * 2**20
SELECT_HEADS_PER_ITER = 4

def _cparams(semantics, vmem_bytes, flags=None):
    limit = max(vmem_bytes + SPILL_AND_TEMP_BYTES, 16 * 2**20)
    return pltpu.CompilerParams(
        dimension_semantics=semantics,
        vmem_limit_bytes=int(min(limit, V7X_VMEM_LIMIT_CAP)),
        flags=flags,
    )


def _nbytes(shape, dtype):
    n = 1
    for s in shape:
        n *= s
    return n * jnp.dtype(dtype).itemsize


def _layer_norm_rows(z, g, b):
    mu = jnp.mean(z, axis=-1, keepdims=True)
    zc = z - mu
    var = jnp.mean(zc * zc, axis=-1, keepdims=True)
    return zc * lax.rsqrt(var + LN_EPS) * g + b


_NT = (((1,), (1,)), ((), ()))


def _rope_table_kernel(pos_ref, freq_ref, sign_ref, cos_ref, sin_ref):
    ang = pos_ref[...].astype(F32) * freq_ref[...]
    cos_ref[...] = jnp.cos(ang)
    sin_ref[...] = jnp.sin(ang) * sign_ref[...]


def _rope_tables(positions):
    T = positions.size
    half = ROT_DIM // 2
    inv_freq = ROPE_THETA ** (-jnp.arange(half, dtype=F32) / half)
    zeros = jnp.zeros((HEAD_DIM - ROT_DIM,), F32)
    freq = jnp.concatenate([inv_freq, inv_freq, zeros])[None, :]
    sign = jnp.concatenate([-jnp.ones((half,), F32), jnp.ones((half,), F32), zeros])[None, :]
    tm = min(1024, T)
    row = pl.BlockSpec((1, HEAD_DIM), lambda i: (0, 0))
    tab = pl.BlockSpec((tm, HEAD_DIM), lambda i: (i, 0))
    return pl.pallas_call(
        _rope_table_kernel,
        out_shape=(jax.ShapeDtypeStruct((T, HEAD_DIM), F32),) * 2,
        grid=(T // tm,),
        in_specs=[pl.BlockSpec((tm, 1), lambda i: (i, 0)), row, row],
        out_specs=(tab, tab),
        compiler_params=_cparams(("parallel",), 8 * _nbytes((tm, HEAD_DIM), F32)),
        name="rope_table",
    )(positions.reshape(T, 1), freq, sign)


def _adaln_kernel(c_ref, w_ref, b_ref, o_ref):
    c = c_ref[...]
    o_ref[...] = jnp.dot(c * jax.nn.sigmoid(c), w_ref[...], preferred_element_type=F32) + b_ref[...]


def _adaln(c, w_ada, b_ada):
    L, D, N = w_ada.shape
    B = c.shape[0]
    rows = 8
    c_pad = jnp.zeros((rows, D), F32).at[:B].set(c)
    tn = min(512, N)
    mod = pl.pallas_call(
        _adaln_kernel,
        out_shape=jax.ShapeDtypeStruct((L, rows, N), F32),
        grid=(L, N // tn),
        in_specs=[
            pl.BlockSpec((rows, D), lambda l, j: (0, 0)),
            pl.BlockSpec((None, D, tn), lambda l, j: (l, 0, j)),
            pl.BlockSpec((None, 1, tn), lambda l, j: (l, 0, j)),
        ],
        out_specs=pl.BlockSpec((None, rows, tn), lambda l, j: (l, 0, j)),
        compiler_params=_cparams(("parallel", "parallel"), 2 * _nbytes((D, tn), F32) + 2**22),
        name="adaln",
    )(c_pad, w_ada, b_ada.reshape(L, 1, N))
    return mod[:, :B].reshape(L * B * 6, 1, D)


def _modulate_kernel(x_ref, sc_ref, sh_ref, o_ref):
    o_ref[...] = (x_ref[...] * (1.0 + sc_ref[...]) + sh_ref[...]).astype(o_ref.dtype)


def _modulate(x, modr, sc_row, sh_row):
    B, S, D = x.shape
    ts = min(512, S)
    nst = S // ts
    return pl.pallas_call(
        _modulate_kernel,
        out_shape=jax.ShapeDtypeStruct((B * S, D), BF16),
        grid=(B, nst),
        in_specs=[
            pl.BlockSpec((None, ts, D), lambda b, s: (b, s, 0)),
            pl.BlockSpec((None, 1, D), lambda b, s: (sc_row(b), 0, 0)),
            pl.BlockSpec((None, 1, D), lambda b, s: (sh_row(b), 0, 0)),
        ],
        out_specs=pl.BlockSpec((ts, D), lambda b, s: (b * nst + s, 0)),
        compiler_params=_cparams(("parallel", "parallel"), 3 * _nbytes((ts, D), F32) + 2**22),
        name="modulate",
    )(x, modr, modr)


M_SPLIT = 2


def _row_chunks(tm):
    rows = tm // M_SPLIT
    return [slice(c * rows, (c + 1) * rows) for c in range(M_SPLIT)]


def _cast_weights_once(w_ref, wb_ref):
    @pl.when(pl.program_id(1) == 0)
    def _():
        wb_ref[...] = w_ref[...].astype(wb_ref.dtype)


def _proj_ac_kernel(a_ref, w_ref, o_ref, wb_ref, *, n_gelu):
    j = pl.program_id(0)
    _cast_weights_once(w_ref, wb_ref)

    @pl.when(j < n_gelu)
    def _():
        for rs in _row_chunks(a_ref.shape[0]):
            acc = jnp.dot(a_ref[rs, :], wb_ref[...], preferred_element_type=F32)
            o_ref[rs, :] = jax.nn.gelu(acc).astype(o_ref.dtype)

    @pl.when(j >= n_gelu)
    def _():
        for rs in _row_chunks(a_ref.shape[0]):
            acc = jnp.dot(a_ref[rs, :], wb_ref[...], preferred_element_type=F32)
            o_ref[rs, :] = acc.astype(o_ref.dtype)


def _proj_vmem(tm, D, tn, out_dtype):
    return 2 * (_nbytes((tm, D), BF16) + _nbytes((D, tn), F32) + _nbytes((tm, tn), out_dtype)) \
        + _nbytes((D, tn), BF16) + 4 * _nbytes((tm, tn), F32)


def _proj_ac(h, w_in, layer, mix):
    T, D = h.shape
    tm = min(1024, T)
    tn = min(512, mix)
    n_gelu = 2 * mix // tn
    skip = 9 * mix // tn

    def w_col(j, i):
        return (layer, 0, jnp.where(j < n_gelu, j, j + skip))

    return pl.pallas_call(
        functools.partial(_proj_ac_kernel, n_gelu=n_gelu),
        out_shape=jax.ShapeDtypeStruct((T, 5 * mix), BF16),
        grid=(5 * mix // tn, T // tm),
        in_specs=[pl.BlockSpec((tm, D), lambda j, i: (i, 0)), pl.BlockSpec((None, D, tn), w_col)],
        out_specs=pl.BlockSpec((tm, tn), lambda j, i: (i, j)),
        scratch_shapes=[pltpu.VMEM((D, tn), BF16)],
        compiler_params=_cparams(("arbitrary", "arbitrary"), _proj_vmem(tm, D, tn, BF16)),
        name="proj_ac",
    )(h, w_in)


def _store_heads(o_ref, rs, val):
    for hh in range(val.shape[1] // HEAD_DIM):
        o_ref[hh, rs, :] = val[:, hh * HEAD_DIM:(hh + 1) * HEAD_DIM]


def _proj_qkv_kernel(a_ref, w_ref, cos_ref, sin_ref, o_ref, wb_ref, *, n_rope):
    j = pl.program_id(0)
    tn = wb_ref.shape[1]
    _cast_weights_once(w_ref, wb_ref)

    @pl.when(j < n_rope)
    def _():
        reps = tn // HEAD_DIM
        half = ROT_DIM // 2
        for rs in _row_chunks(a_ref.shape[0]):
            acc = jnp.dot(a_ref[rs, :], wb_ref[...], preferred_element_type=F32)
            cos = jnp.concatenate([cos_ref[rs, :]] * reps, axis=1)
            sin = jnp.concatenate([sin_ref[rs, :]] * reps, axis=1)
            lane = lax.broadcasted_iota(jnp.int32, acc.shape, 1) % HEAD_DIM
            partner = jnp.where(lane < half, pltpu.roll(acc, tn - half, 1), pltpu.roll(acc, half, 1))
            _store_heads(o_ref, rs, acc * cos + partner * sin)

    @pl.when(j >= n_rope)
    def _():
        for rs in _row_chunks(a_ref.shape[0]):
            _store_heads(o_ref, rs, jnp.dot(a_ref[rs, :], wb_ref[...], preferred_element_type=F32))


def _proj_qkv(h, w_in, layer, cos_t, sin_t, mix):
    T, D = h.shape
    tm = min(1024, T)
    tn = min(512, mix)
    hpt = tn // HEAD_DIM
    first = 2 * mix // tn
    tab = pl.BlockSpec((tm, HEAD_DIM), lambda j, i: (i, 0))
    return pl.pallas_call(
        functools.partial(_proj_qkv_kernel, n_rope=6 * mix // tn),
        out_shape=jax.ShapeDtypeStruct((9 * mix // HEAD_DIM, T, HEAD_DIM), F32),
        grid=(9 * mix // tn, T // tm),
        in_specs=[
            pl.BlockSpec((tm, D), lambda j, i: (i, 0)),
            pl.BlockSpec((None, D, tn), lambda j, i: (layer, 0, first + j)),
            tab, tab,
        ],
        out_specs=pl.BlockSpec((hpt, tm, HEAD_DIM), lambda j, i: (j, i, 0)),
        scratch_shapes=[pltpu.VMEM((D, tn), BF16)],
        compiler_params=_cparams(("arbitrary", "arbitrary"),
                                 _proj_vmem(tm, D, tn, F32) + 4 * _nbytes((tm, HEAD_DIM), F32)),
        name="proj_qkv",
    )(h, w_in, cos_t, sin_t)


def _gmlp_kernel(u_ref, v_ref, g_ref, b_ref, wsp_ref, bspt_ref, o_ref, *, groups):
    tg = u_ref.shape[0]
    vn = _layer_norm_rows(v_ref[...].astype(F32), g_ref[...], b_ref[...]).astype(BF16)
    t_idx = lax.broadcasted_iota(jnp.int32, (CHUNK, CHUNK), 0)
    s_idx = lax.broadcasted_iota(jnp.int32, (CHUNK, CHUNK), 1)
    causal = s_idx <= t_idx
    for g in range(groups):
        cols = slice(g * CHUNK, (g + 1) * CHUNK)
        w = jnp.where(causal, wsp_ref[g], 0.0).astype(BF16)
        bias = bspt_ref[:, g:g + 1]
        for c in range(tg // CHUNK):
            rows = slice(c * CHUNK, (c + 1) * CHUNK)
            mixed = jnp.dot(w, vn[rows, cols], preferred_element_type=F32) + bias
            o_ref[rows, cols] = (u_ref[rows, cols].astype(F32) * mixed).astype(o_ref.dtype)


def _gmlp(proj, ln_g, ln_b, w_sp, b_sp, mix):
    T = proj.shape[0]
    groups = w_sp.shape[0]
    tg = min(512, T)
    row = pl.BlockSpec((1, mix), lambda i: (0, 0))
    return pl.pallas_call(
        functools.partial(_gmlp_kernel, groups=groups),
        out_shape=jax.ShapeDtypeStruct((T, mix), BF16),
        grid=(T // tg,),
        in_specs=[
            pl.BlockSpec((tg, mix), lambda i: (i, 0)),
            pl.BlockSpec((tg, mix), lambda i: (i, 1)),
            row, row,
            pl.BlockSpec((groups, CHUNK, CHUNK), lambda i: (0, 0, 0)),
            pl.BlockSpec((CHUNK, groups), lambda i: (0, 0)),
        ],
        out_specs=pl.BlockSpec((tg, mix), lambda i: (i, 0)),
        compiler_params=_cparams(("parallel",), 10 * _nbytes((tg, mix), F32)),
        name="gmlp",
    )(proj, proj, ln_g.reshape(1, mix), ln_b.reshape(1, mix), w_sp, b_sp.T)


def _attn_kernel(q_ref, kc_ref, kp_ref, vc_ref, vp_ref, o_ref, lse_ref, *, dilation, periods_per_seq):
    has_prev = (pl.program_id(0) % periods_per_seq) > 0
    qi = lax.broadcasted_iota(jnp.int32, (ATTN_BLOCK, ATTN_BLOCK), 0)
    kj = lax.broadcasted_iota(jnp.int32, (ATTN_BLOCK, ATTN_BLOCK), 1)
    mask_c = kj <= qi
    mask_p = (kj >= qi) & has_prev
    scale = HEAD_DIM ** -0.5

    def one_class(r, carry):
        rows = pl.ds(r, ATTN_BLOCK, stride=dilation) if dilation > 1 else pl.ds(0, ATTN_BLOCK)
        for h in range(q_ref.shape[0]):
            q = q_ref[h, rows, :].astype(BF16)
            s_c = lax.dot_general(q, kc_ref[h, rows, :].astype(BF16), _NT, preferred_element_type=F32) * scale
            s_p = lax.dot_general(q, kp_ref[h, rows, :].astype(BF16), _NT, preferred_element_type=F32) * scale
            s_c = jnp.where(mask_c, s_c, MASKED)
            s_p = jnp.where(mask_p, s_p, MASKED)
            m = jnp.maximum(jnp.max(s_c, axis=-1, keepdims=True), jnp.max(s_p, axis=-1, keepdims=True))
            p_c = jnp.exp(s_c - m)
            p_p = jnp.exp(s_p - m)
            l = jnp.sum(p_c, axis=-1, keepdims=True) + jnp.sum(p_p, axis=-1, keepdims=True)
            o = jnp.dot(p_c.astype(BF16), vc_ref[h, rows, :].astype(BF16), preferred_element_type=F32)
            o = o + jnp.dot(p_p.astype(BF16), vp_ref[h, rows, :].astype(BF16), preferred_element_type=F32)
            o_ref[h, rows, :] = o / l
            lse_ref[h, rows, :] = jnp.broadcast_to(m + jnp.log(l), (ATTN_BLOCK, HEAD_DIM))
        return carry

    if dilation > 1:
        lax.fori_loop(0, dilation, one_class, 0)
    else:
        one_class(0, 0)


def _dilated_attention(qkv, group, dilation, seq, heads):
    T = qkv.shape[1]
    period = ATTN_BLOCK * dilation
    periods_per_seq = seq // period
    hps = heads
    while 14 * _nbytes((hps, period, HEAD_DIM), F32) > ATTN_VMEM_BUDGET and hps % 2 == 0:
        hps //= 2

    def cur(part):
        return pl.BlockSpec((hps, period, HEAD_DIM), lambda p, h: ((3 * part + group) * (heads // hps) + h, p, 0))

    def prev(part):
        return pl.BlockSpec((hps, period, HEAD_DIM),
                            lambda p, h: ((3 * part + group) * (heads // hps) + h, jnp.maximum(p - 1, 0), 0))

    out = pl.BlockSpec((hps, period, HEAD_DIM), lambda p, h: (h, p, 0))
    shape = jax.ShapeDtypeStruct((heads, T, HEAD_DIM), F32)
    return pl.pallas_call(
        functools.partial(_attn_kernel, dilation=dilation, periods_per_seq=periods_per_seq),
        out_shape=(shape, shape),
        grid=(T // period, heads // hps),
        in_specs=[cur(0), cur(1), prev(1), cur(2), prev(2)],
        out_specs=(out, out),
        compiler_params=_cparams(("parallel", "parallel"), 14 * _nbytes((hps, period, HEAD_DIM), F32)),
        name=f"dilated_attention_d{dilation}",
    )(qkv, qkv, qkv, qkv, qkv)


def _combine_kernel(o0, o1, o2, l0, l1, l2, out_ref):
    for h in range(o0.shape[0]):
        a, b, c = l0[h], l1[h], l2[h]
        m = jnp.maximum(jnp.maximum(a, b), c)
        ea, eb, ec = jnp.exp(a - m), jnp.exp(b - m), jnp.exp(c - m)
        num = ea * o0[h] + eb * o1[h] + ec * o2[h]
        out_ref[:, h * HEAD_DIM:(h + 1) * HEAD_DIM] = (num / (ea + eb + ec)).astype(out_ref.dtype)


def _combine_groups(outs, lses):
    heads, T, _ = outs[0].shape
    tm = min(512, T)
    blk = pl.BlockSpec((heads, tm, HEAD_DIM), lambda i: (0, i, 0))
    return pl.pallas_call(
        _combine_kernel,
        out_shape=jax.ShapeDtypeStruct((T, heads * HEAD_DIM), BF16),
        grid=(T // tm,),
        in_specs=[blk] * 6,
        out_specs=pl.BlockSpec((tm, heads * HEAD_DIM), lambda i: (i, 0)),
        compiler_params=_cparams(("parallel",), 14 * _nbytes((heads, tm, HEAD_DIM), F32)),
        name="attn_combine",
    )(*outs, *lses)


def _conv_kernel(gb_ref, gc_ref, xin_ref, gcp_ref, xinp_ref, cw_ref, o_ref, *, tiles_per_seq):
    first = (pl.program_id(0) % tiles_per_seq) == 0
    z = gc_ref[...].astype(F32) * xin_ref[...].astype(F32)
    zp = gcp_ref[...].astype(F32) * xinp_ref[...].astype(F32)
    zp = jnp.where(first, 0.0, zp)
    last = zp.shape[0] - 1
    rows = lax.broadcasted_iota(jnp.int32, z.shape, 0)
    z1 = jnp.where(rows == 0, zp[last:last + 1], pltpu.roll(z, 1, 0))
    z2 = jnp.where(rows == 0, zp[last - 1:last], jnp.where(rows == 1, zp[last:last + 1], pltpu.roll(z, 2, 0)))
    y = cw_ref[0:1, :] * z2 + cw_ref[1:2, :] * z1 + cw_ref[2:3, :] * z
    o_ref[...] = (gb_ref[...].astype(F32) * y).astype(o_ref.dtype)


def _short_conv(proj, conv_w, seq, mix, col0):
    T = proj.shape[0]
    tc = min(512, seq)
    tiles_per_seq = seq // tc
    halo = V7X_BF16_SUBLANES
    cb = col0 // mix
    per_halo = tc // halo

    def cur(k):
        return pl.BlockSpec((tc, mix), lambda i: (i, cb + k))

    def prev(k):
        return pl.BlockSpec((halo, mix), lambda i: (jnp.maximum(i * per_halo - 1, 0), cb + k))

    return pl.pallas_call(
        functools.partial(_conv_kernel, tiles_per_seq=tiles_per_seq),
        out_shape=jax.ShapeDtypeStruct((T, mix), BF16),
        grid=(T // tc,),
        in_specs=[cur(0), cur(1), cur(2), prev(1), prev(2),
                  pl.BlockSpec(conv_w.shape, lambda i: (0, 0))],
        out_specs=pl.BlockSpec((tc, mix), lambda i: (i, 0)),
        compiler_params=_cparams(("parallel",), 16 * _nbytes((tc, mix), F32)),
        name="short_conv",
    )(proj, proj, proj, proj, proj, conv_w)


def _merge_kernel(h_ref, ba_ref, bb_ref, bc_ref, wg_ref, bg_ref, wb_ref, o_ref, wgb_ref):
    _cast_weights_once(wg_ref, wgb_ref)
    h = h_ref[...]
    total = None
    for g, br in enumerate((ba_ref, bb_ref, bc_ref)):
        gate = jax.nn.sigmoid(jnp.dot(h, wgb_ref[g], preferred_element_type=F32) + bg_ref[g])
        term = gate * jnp.dot(br[...], wb_ref[g], preferred_element_type=F32)
        total = term if total is None else total + term
    o_ref[...] = total.astype(o_ref.dtype)


def _merge(h, branches, w_gate, layer, b_gate, w_branch):
    T, D = h.shape
    mix = branches[0].shape[1]
    tm = min(512, T)
    tn = min(256, D)
    br = pl.BlockSpec((tm, mix), lambda j, i: (i, 0))
    need = 2 * (_nbytes((tm, D), BF16) + 3 * _nbytes((tm, mix), BF16) + 3 * _nbytes((D, tn), F32)
                + 3 * _nbytes((mix, tn), BF16) + _nbytes((tm, tn), BF16)) \
        + 3 * _nbytes((D, tn), BF16) + 8 * _nbytes((tm, tn), F32)
    return pl.pallas_call(
        _merge_kernel,
        out_shape=jax.ShapeDtypeStruct((T, D), BF16),
        grid=(D // tn, T // tm),
        in_specs=[
            pl.BlockSpec((tm, D), lambda j, i: (i, 0)),
            br, br, br,
            pl.BlockSpec((None, N_BRANCH, D, tn), lambda j, i: (layer, 0, 0, j)),
            pl.BlockSpec((N_BRANCH, 1, tn), lambda j, i: (0, 0, j)),
            pl.BlockSpec((None, N_BRANCH, mix, tn), lambda j, i: (layer, 0, 0, j)),
        ],
        out_specs=pl.BlockSpec((tm, tn), lambda j, i: (i, j)),
        scratch_shapes=[pltpu.VMEM((N_BRANCH, D, tn), BF16)],
        compiler_params=_cparams(("arbitrary", "arbitrary"), need),
        name="branch_merge",
    )(h, *branches, w_gate, b_gate.reshape(N_BRANCH, 1, D), w_branch)


def _matmul_kernel(a_ref, w_ref, o_ref, wb_ref):
    _cast_weights_once(w_ref, wb_ref)
    o_ref[...] = jnp.dot(a_ref[...], wb_ref[...], preferred_element_type=F32).astype(o_ref.dtype)


def _matmul(a, w_stack, layer, out_dtype, name):
    T, K = a.shape
    N = w_stack.shape[2]
    tm = min(1024, T)
    tn = min(512, N)
    return pl.pallas_call(
        _matmul_kernel,
        out_shape=jax.ShapeDtypeStruct((T, N), out_dtype),
        grid=(N // tn, T // tm),
        in_specs=[pl.BlockSpec((tm, K), lambda j, i: (i, 0)),
                  pl.BlockSpec((None, K, tn), lambda j, i: (layer, 0, j))],
        out_specs=pl.BlockSpec((tm, tn), lambda j, i: (i, j)),
        scratch_shapes=[pltpu.VMEM((K, tn), BF16)],
        compiler_params=_cparams(("arbitrary", "arbitrary"), _proj_vmem(tm, K, tn, out_dtype)),
        name=name,
    )(a, w_stack)


def _residual_ln_kernel(x_ref, y_ref, gate_ref, lg_ref, lb_ref, *rest, alpha, y_transposed, emit_h, emit_ht):
    y = y_ref[...].astype(F32)
    if y_transposed:
        y = y.T
    xn = _layer_norm_rows(alpha * x_ref[...] + gate_ref[...] * y, lg_ref[...], lb_ref[...])
    if emit_h:
        sc_ref, sh_ref, xo_ref, ho_ref = rest[:4]
        h = xn * (1.0 + sc_ref[...]) + sh_ref[...]
        ho_ref[...] = h.astype(ho_ref.dtype)
        if emit_ht:
            rest[4][...] = h.T.astype(ho_ref.dtype)
    else:
        (xo_ref,) = rest
    xo_ref[...] = xn


def _residual_ln(x, y, modr, gate_row, ln_g, ln_b, alpha, *, y_transposed, next_rows=None, emit_ht=False):
    B, S, D = x.shape
    ts = min(256, S)
    nst = S // ts
    emit_h = next_rows is not None

    def mod_spec(row_fn):
        return pl.BlockSpec((None, 1, D), lambda b, s: (row_fn(b), 0, 0))

    vec = pl.BlockSpec((1, D), lambda b, s: (0, 0))
    if y_transposed:
        y_spec = pl.BlockSpec((D, ts), lambda b, s: (0, b * nst + s))
    else:
        y_spec = pl.BlockSpec((ts, D), lambda b, s: (b * nst + s, 0))
    x_spec = pl.BlockSpec((None, ts, D), lambda b, s: (b, s, 0))
    in_specs = [x_spec, y_spec, mod_spec(gate_row), vec, vec]
    args = [x, y, modr, ln_g.reshape(1, D), ln_b.reshape(1, D)]
    out_shape = [jax.ShapeDtypeStruct((B, S, D), F32)]
    out_specs = [x_spec]
    if emit_h:
        in_specs += [mod_spec(next_rows[0]), mod_spec(next_rows[1])]
        args += [modr, modr]
        out_shape.append(jax.ShapeDtypeStruct((B * S, D), BF16))
        out_specs.append(pl.BlockSpec((ts, D), lambda b, s: (b * nst + s, 0)))
        if emit_ht:
            out_shape.append(jax.ShapeDtypeStruct((D, B * S), BF16))
            out_specs.append(pl.BlockSpec((D, ts), lambda b, s: (0, b * nst + s)))
    res = pl.pallas_call(
        functools.partial(_residual_ln_kernel, alpha=alpha, y_transposed=y_transposed, emit_h=emit_h,
                          emit_ht=emit_ht),
        out_shape=tuple(out_shape),
        grid=(B, nst),
        in_specs=in_specs,
        out_specs=tuple(out_specs),
        compiler_params=_cparams(("parallel", "parallel"), 14 * _nbytes((ts, D), F32)),
        name="residual_ln",
    )(*args)
    return tuple(res) + (None,) * (3 - len(res))


def _peer_scores_kernel(h_ref, wpq_ref, keys_ref, st_ref):
    q = jnp.dot(h_ref[...], wpq_ref[...], preferred_element_type=F32).astype(BF16)
    st_ref[...] = lax.dot_general(keys_ref[...], q, _NT, preferred_element_type=F32)


def _peer_scores(h, w_pq, keys_bd):
    T, D = h.shape
    HQ = w_pq.shape[1]
    R = keys_bd.shape[0]
    tm = min(512, T)
    need = 2 * (_nbytes((tm, D), BF16) + _nbytes((D, HQ), BF16) + _nbytes((R, HQ), BF16)
                + _nbytes((R, tm), F32)) + 2 * _nbytes((tm, HQ), F32)
    return pl.pallas_call(
        _peer_scores_kernel,
        out_shape=jax.ShapeDtypeStruct((R, T), F32),
        grid=(T // tm,),
        in_specs=[
            pl.BlockSpec((tm, D), lambda i: (i, 0)),
            pl.BlockSpec((D, HQ), lambda i: (0, 0)),
            pl.BlockSpec((R, HQ), lambda i: (0, 0)),
        ],
        out_specs=pl.BlockSpec((R, tm), lambda i: (0, i)),
        compiler_params=_cparams(("parallel",), need),
        name="peer_scores",
    )(h, w_pq, keys_bd)


def _top16_ranks(s, row_ids, exact):
    n = s.shape[0]
    k_ids = lax.broadcasted_iota(jnp.int32, (PEER_TOPK, s.shape[1]), 0)
    rank = jnp.full(s.shape, float(PEER_TOPK), F32)
    tops = jnp.zeros((PEER_TOPK, s.shape[1]), F32)
    for k in range(PEER_TOPK):
        m = jnp.max(s, axis=0, keepdims=True)
        sel = s == m
        if exact:
            first = jnp.min(jnp.where(sel, row_ids, float(n)), axis=0, keepdims=True)
            sel = row_ids == first
        rank = jnp.where(sel, float(k), rank)
        s = jnp.where(sel, -jnp.inf, s)
        tops = jnp.where(k_ids == k, m, tops)
    count = jnp.sum(jnp.where(rank < float(PEER_TOPK), 1.0, 0.0), axis=0, keepdims=True)
    return rank, tops, count


def _pair_counts(t1, t2, exact):
    L = t1.shape[1]
    K = PEER_TOPK
    sub = 8
    pieces = [t1[0:1] + t2]
    pos = [lax.broadcasted_iota(jnp.int32, (K, L), 0).astype(F32)]
    r8 = lax.broadcasted_iota(jnp.int32, (sub, L), 0).astype(F32)
    for a in range(1, sub):
        pieces.append(t1[a:a + 1] + t2[0:sub])
        pos.append(r8 + float(a * K))
    pieces.append(t1[sub:K] + t2[0:1])
    pos.append((r8 + float(sub)) * float(K))
    v = jnp.concatenate(pieces, axis=0)
    p = jnp.concatenate(pos, axis=0)
    picked = jnp.zeros(v.shape, F32)
    z = jnp.zeros((1, L), F32)
    top = None
    for k in range(K):
        m = jnp.max(v, axis=0, keepdims=True)
        sel = v == m
        if exact:
            first = jnp.min(jnp.where(sel, p, float(K * K)), axis=0, keepdims=True)
            sel = p == first
        picked = jnp.where(sel, 1.0, picked)
        v = jnp.where(sel, -jnp.inf, v)
        if k == 0:
            top = m
            z = z + 1.0
        else:
            z = z + jnp.exp(m - top)
    a_ids = lax.broadcasted_iota(jnp.int32, (sub, L), 0)
    low = jnp.zeros((sub, L), F32)
    low = jnp.where(a_ids == 0, jnp.sum(picked[0:K], axis=0, keepdims=True), low)
    for a in range(1, sub):
        off = K + (a - 1) * sub
        low = jnp.where(a_ids == a, jnp.sum(picked[off:off + sub], axis=0, keepdims=True), low)
    counts = jnp.concatenate([low, picked[K + (sub - 1) * sub:]], axis=0)
    return counts, z, jnp.sum(picked, axis=0, keepdims=True)


def _peer_select_kernel(st_ref, rank2_ref, cnt1_ref, e1_ref, e2_ref, *, heads, nk):
    tt = st_ref.shape[1]
    row_ids = lax.broadcasted_iota(jnp.int32, (nk, V7X_LANES), 0).astype(F32)
    full = float(PEER_TOPK)

    hpi = SELECT_HEADS_PER_ITER if heads % SELECT_HEADS_PER_ITER == 0 else 1

    def per_iter(idx, carry):
        c = idx // (heads // hpi)
        lanes = pl.ds(pl.multiple_of(c * V7X_LANES, V7X_LANES), V7X_LANES)

        def select(exact):
            flags = []
            for k in range(hpi):
                hh = (idx % (heads // hpi)) * hpi + k
                r1 = pl.ds(pl.multiple_of((2 * hh) * nk, nk), nk)
                r2 = pl.ds(pl.multiple_of((2 * hh + 1) * nk, nk), nk)
                ro = pl.ds(pl.multiple_of(hh * nk, nk), nk)
                s1 = st_ref[r1, lanes]
                s2 = st_ref[r2, lanes]
                rank1, t1, n1 = _top16_ranks(s1, row_ids, exact)
                rank2, t2, n2 = _top16_ranks(s2, row_ids, exact)
                counts, z, n3 = _pair_counts(t1, t2, exact)
                cnt1 = jnp.zeros((nk, V7X_LANES), F32)
                for a in range(PEER_TOPK):
                    cnt1 = jnp.where(rank1 == float(a), counts[a:a + 1], cnt1)
                rank2_ref[ro, lanes] = rank2.astype(rank2_ref.dtype)
                cnt1_ref[ro, lanes] = cnt1
                e1_ref[ro, lanes] = jnp.exp(s1 - t1[0:1]) / z
                e2_ref[ro, lanes] = jnp.exp(s2 - t2[0:1]).astype(e2_ref.dtype)
                flags.append(jnp.where((n1 == full) & (n2 == full) & (n3 == full), 0.0, 1.0))
            return functools.reduce(jnp.maximum, flags)

        tied = jnp.max(select(exact=False)) > 0.0

        @pl.when(tied)
        def _():
            select(exact=True)

        return carry

    lax.fori_loop(0, (tt // V7X_LANES) * (heads // hpi), per_iter, 0)


def _peer_select(st, heads, nk):
    R, T = st.shape
    tt = min(512, T)
    blk = pl.BlockSpec((heads * nk, tt), lambda i: (0, i))
    return pl.pallas_call(
        functools.partial(_peer_select_kernel, heads=heads, nk=nk),
        out_shape=tuple(jax.ShapeDtypeStruct((heads * nk, T), dt) for dt in (BF16, F32, F32, BF16)),
        grid=(T // tt,),
        in_specs=[pl.BlockSpec((R, tt), lambda i: (0, i))],
        out_specs=(blk,) * 4,
        compiler_params=_cparams(("parallel",), 2 * _nbytes((R, tt), F32) + 8 * _nbytes((heads * nk, tt), F32)),
        name="peer_select",
    )(st)


def _peer_dense_kernel(ht_ref, wu_ref, wv_ref, rank2_ref, cnt1_ref, e1_ref, e2_ref, yt_ref,
                       a_ref, carry_ref, *, heads, nk, pieces, n_tiles):
    e = pl.program_id(1)

    @pl.when(e == 0)
    def _():
        yt_ref[...] = jnp.zeros_like(yt_ref)
        carry_ref[...] = jnp.zeros_like(carry_ref)

    keys_per_step = sum(pieces)
    tile = jnp.minimum(e, n_tiles - 1)
    ht = ht_ref[...]
    tm = ht.shape[1]
    pk = V7X_BF16_SUBLANES
    zero = jnp.zeros((), BF16)
    key0 = [sum(pieces[:p]) for p in range(len(pieces))]
    acts = [jnp.dot(wu_ref[key0[p] * nk:(key0[p] + pieces[p]) * nk, :], ht, preferred_element_type=F32)
            for p in range(len(pieces))]

    def gated(p, dst_ref, row0):
        for b in range(pieces[p]):
            i1 = tile * keys_per_step + key0[p] + b
            gate = None
            for hh in range(heads):
                row = pl.ds(hh * nk + i1, 1)
                blk = slice(hh * nk, (hh + 1) * nk)
                cnt = jnp.broadcast_to(cnt1_ref[row, :], (pk, tm)).astype(BF16)[None]
                e1 = jnp.broadcast_to(e1_ref[row, :], (pk, tm)).astype(BF16)[None]
                r2 = rank2_ref[blk, :].reshape(nk // pk, pk, tm)
                e2 = e2_ref[blk, :].reshape(nk // pk, pk, tm)
                term = jnp.where(r2 < cnt, e2 * e1, zero)
                gate = term if gate is None else gate + term
            src = slice(b * nk, (b + 1) * nk)
            dst = slice(row0 + b * nk, row0 + (b + 1) * nk)
            dst_ref[dst, :] = jax.nn.gelu(acts[p][src, :]).astype(BF16) * gate.reshape(nk, tm)

    carried = pieces[-1] * nk
    a_ref[0:carried, :] = carry_ref[...]
    for p in range(len(pieces) - 1):
        gated(p, a_ref, carried + key0[p] * nk)
    yt_ref[...] += jnp.dot(wv_ref[...], a_ref[...], preferred_element_type=F32)
    gated(len(pieces) - 1, carry_ref, 0)


def _peer_pieces(keys_per_step):
    if keys_per_step >= 4:
        return (keys_per_step - 2, 1, 1)
    return (keys_per_step - 1, 1)


def _peer_dense(ht, w_u, layer, w_vt_padded, sel, heads, nk):
    D, T = ht.shape
    E = w_u.shape[1]
    tm = min(512, T)
    te = _peer_tile(nk)
    pieces = _peer_pieces(te // nk)
    n_tiles = E // te
    sel_spec = pl.BlockSpec((heads * nk, tm), lambda i, e: (0, i), pipeline_mode=pl.Buffered(1))
    need = _nbytes((tm, D), BF16) + 4 * _nbytes((te, D), BF16) + 2 * _nbytes((D, tm), F32) \
        + 4 * _nbytes((heads * nk, tm), F32) + 2 * _nbytes((te, tm), BF16) + 4 * _nbytes((te, tm), F32)
    return pl.pallas_call(
        functools.partial(_peer_dense_kernel, heads=heads, nk=nk, pieces=pieces, n_tiles=n_tiles),
        out_shape=jax.ShapeDtypeStruct((D, T), F32),
        grid=(T // tm, n_tiles + 1),
        in_specs=[
            pl.BlockSpec((D, tm), lambda i, e: (0, i), pipeline_mode=pl.Buffered(1)),
            pl.BlockSpec((None, te, D), lambda i, e: (layer, jnp.minimum(e, n_tiles - 1), 0)),
            pl.BlockSpec((D, te), lambda i, e: (0, e)),
            sel_spec, sel_spec, sel_spec, sel_spec,
        ],
        out_specs=pl.BlockSpec((D, tm), lambda i, e: (0, i)),
        scratch_shapes=[pltpu.VMEM((te, tm), BF16), pltpu.VMEM((pieces[-1] * nk, tm), BF16)],
        compiler_params=_cparams(("parallel", "arbitrary"), need),
        name="peer_dense",
    )(ht, w_u, w_vt_padded, *sel)


def _peer_tile(nk):
    return min(4, nk) * nk


def _transpose_pad_kernel(w_ref, o_ref, *, n_blocks):
    j = pl.program_id(0)
    inside = (j > 0) & (j <= n_blocks)

    @pl.when(inside)
    def _():
        o_ref[...] = w_ref[...].T.astype(o_ref.dtype)

    @pl.when(jnp.logical_not(inside))
    def _():
        o_ref[...] = jnp.zeros_like(o_ref)


def _transpose_pad(w_v, front, back):
    E, D = w_v.shape
    n_blocks = E // front
    assert back % front == 0
    return pl.pallas_call(
        functools.partial(_transpose_pad_kernel, n_blocks=n_blocks),
        out_shape=jax.ShapeDtypeStruct((D, front + E + back), BF16),
        grid=(1 + n_blocks + back // front,),
        in_specs=[pl.BlockSpec((front, D), lambda j: (jnp.clip(j - 1, 0, n_blocks - 1), 0))],
        out_specs=pl.BlockSpec((D, front), lambda j: (0, j)),
        compiler_params=_cparams(("parallel",), 6 * _nbytes((front, D), F32)),
        name="transpose_pad_wv",
    )(w_v)


def _mixer_sublayer(h, w_in, layer, cos_t, sin_t, w_gate, b_gate, ln_v_g, ln_v_b, w_sp, b_sp, conv_w,
                    w_branch, w_o, seq):
    mix = ln_v_g.shape[0]
    heads = mix // HEAD_DIM
    proj_ac = _proj_ac(h, w_in, layer, mix)
    qkv = _proj_qkv(h, w_in, layer, cos_t, sin_t, mix)
    branch_a = _gmlp(proj_ac, ln_v_g, ln_v_b, w_sp, b_sp, mix)
    outs, lses = [], []
    for g, (_, dilation) in enumerate(B_PATTERNS):
        o, lse = _dilated_attention(qkv, g, dilation, seq, heads)
        outs.append(o)
        lses.append(lse)
    branch_b = _combine_groups(outs, lses)
    branch_c = _short_conv(proj_ac, conv_w, seq, mix, 2 * mix)
    merged = _merge(h, (branch_a, branch_b, branch_c), w_gate, layer, b_gate, w_branch)
    return _matmul(merged, w_o, layer, BF16, "out_proj")


def _peer_sublayer(h, ht, w_pq, sub_keys, w_u, layer, w_v):
    heads, _, nk, _ = sub_keys.shape
    keys_bd = jax.scipy.linalg.block_diag(
        *[sub_keys[hh, p] for hh in range(heads) for p in range(2)]).astype(BF16)
    te = _peer_tile(nk)
    carried = _peer_pieces(te // nk)[-1] * nk
    w_vt_padded = _transpose_pad(w_v, carried, te - carried)
    st = _peer_scores(h, w_pq.astype(BF16), keys_bd)
    sel = _peer_select(st, heads, nk)
    return _peer_dense(ht, w_u, layer, w_vt_padded, sel, heads, nk)


def kernel(x, c, positions, w_ada, b_ada, w_in, w_gate, b_gate, ln_v_g, ln_v_b, w_sp, b_sp,
           conv_w, w_branch, w_o, ln1_g, ln1_b, w_pq, sub_keys, w_u, w_v, ln2_g, ln2_b):
    B, S, D = x.shape
    depth = w_ada.shape[0]
    alpha = (2 * depth) ** 0.25
    assert S % (ATTN_BLOCK * B_PATTERNS[-1][1]) == 0

    cos_t, sin_t = _rope_tables(positions)
    modr = _adaln(c, w_ada, b_ada)

    def mod_row(layer, k):
        return lambda b: (layer * B + b) * 6 + k

    w_u_b = w_u.astype(BF16)
    w_branch_b = w_branch.astype(BF16)

    h = _modulate(x, modr, mod_row(0, 1), mod_row(0, 0))
    for l in range(depth):
        y = _mixer_sublayer(h, w_in, l, cos_t, sin_t, w_gate, b_gate[l],
                            ln_v_g[l], ln_v_b[l], w_sp[l], b_sp[l], conv_w[l],
                            w_branch_b, w_o, S)
        x, h, ht = _residual_ln(x, y, modr, mod_row(l, 2), ln1_g[l], ln1_b[l], alpha, y_transposed=False,
                                next_rows=(mod_row(l, 4), mod_row(l, 3)), emit_ht=True)
        y_t = _peer_sublayer(h, ht, w_pq[l], sub_keys[l], w_u_b, l, w_v[l])
        nxt = (mod_row(l + 1, 1), mod_row(l + 1, 0)) if l + 1 < depth else None
        x, h, _ = _residual_ln(x, y_t, modr, mod_row(l, 5), ln2_g[l], ln2_b[l], alpha, y_transposed=True,
                               next_rows=nxt)
    return x
```

```python
import functools

import jax
import jax.numpy as jnp
from jax import lax
from jax.experimental import pallas as pl
from jax.experimental.pallas import tpu as pltpu

F32 = jnp.float32
BF16 = jnp.bfloat16

HEAD_DIM = 128
CHUNK = 128
ATTN_BLOCK = 128
ROT_DIM = HEAD_DIM // 4
ROPE_THETA = 500000.0
B_PATTERNS = ((128, 1), (512, 4), (2048, 16))
N_BRANCH = 3
PEER_TOPK = 16
LN_EPS = 1e-5
MASKED = -1e30

V7X_LANES = 128
V7X_BF16_SUBLANES = 16
V7X_VMEM_LIMIT_CAP = 58 * 2**20
SPILL_AND_TEMP_BYTES = 4 * 2**20
ATTN_VMEM_BUDGET = 32 * 2**20
SELECT_HEADS_PER_ITER = 4

def _cparams(semantics, vmem_bytes, flags=None):
    limit = max(vmem_bytes + SPILL_AND_TEMP_BYTES, 16 * 2**20)
    return pltpu.CompilerParams(
        dimension_semantics=semantics,
        vmem_limit_bytes=int(min(limit, V7X_VMEM_LIMIT_CAP)),
        flags=flags,
    )


def _nbytes(shape, dtype):
    n = 1
    for s in shape:
        n *= s
    return n * jnp.dtype(dtype).itemsize


def _layer_norm_rows(z, g, b):
    mu = jnp.mean(z, axis=-1, keepdims=True)
    zc = z - mu
    var = jnp.mean(zc * zc, axis=-1, keepdims=True)
    return zc * lax.rsqrt(var + LN_EPS) * g + b


_NT = (((1,), (1,)), ((), ()))


def _rope_table_kernel(pos_ref, freq_ref, sign_ref, cos_ref, sin_ref):
    ang = pos_ref[...].astype(F32) * freq_ref[...]
    cos_ref[...] = jnp.cos(ang)
    sin_ref[...] = jnp.sin(ang) * sign_ref[...]


def _rope_tables(positions):
    T = positions.size
    half = ROT_DIM // 2
    inv_freq = ROPE_THETA ** (-jnp.arange(half, dtype=F32) / half)
    zeros = jnp.zeros((HEAD_DIM - ROT_DIM,), F32)
    freq = jnp.concatenate([inv_freq, inv_freq, zeros])[None, :]
    sign = jnp.concatenate([-jnp.ones((half,), F32), jnp.ones((half,), F32), zeros])[None, :]
    tm = min(1024, T)
    row = pl.BlockSpec((1, HEAD_DIM), lambda i: (0, 0))
    tab = pl.BlockSpec((tm, HEAD_DIM), lambda i: (i, 0))
    return pl.pallas_call(
        _rope_table_kernel,
        out_shape=(jax.ShapeDtypeStruct((T, HEAD_DIM), F32),) * 2,
        grid=(T // tm,),
        in_specs=[pl.BlockSpec((tm, 1), lambda i: (i, 0)), row, row],
        out_specs=(tab, tab),
        compiler_params=_cparams(("parallel",), 8 * _nbytes((tm, HEAD_DIM), F32)),
        name="rope_table",
    )(positions.reshape(T, 1), freq, sign)


def _adaln_kernel(c_ref, w_ref, b_ref, o_ref):
    c = c_ref[...]
    o_ref[...] = jnp.dot(c * jax.nn.sigmoid(c), w_ref[...], preferred_element_type=F32) + b_ref[...]


def _adaln(c, w_ada, b_ada):
    L, D, N = w_ada.shape
    B = c.shape[0]
    rows = 8
    c_pad = jnp.zeros((rows, D), F32).at[:B].set(c)
    tn = min(512, N)
    mod = pl.pallas_call(
        _adaln_kernel,
        out_shape=jax.ShapeDtypeStruct((L, rows, N), F32),
        grid=(L, N // tn),
        in_specs=[
            pl.BlockSpec((rows, D), lambda l, j: (0, 0)),
            pl.BlockSpec((None, D, tn), lambda l, j: (l, 0, j)),
            pl.BlockSpec((None, 1, tn), lambda l, j: (l, 0, j)),
        ],
        out_specs=pl.BlockSpec((None, rows, tn), lambda l, j: (l, 0, j)),
        compiler_params=_cparams(("parallel", "parallel"), 2 * _nbytes((D, tn), F32) + 2**22),
        name="adaln",
    )(c_pad, w_ada, b_ada.reshape(L, 1, N))
    return mod[:, :B].reshape(L * B * 6, 1, D)


def _modulate_kernel(x_ref, sc_ref, sh_ref, o_ref):
    o_ref[...] = (x_ref[...] * (1.0 + sc_ref[...]) + sh_ref[...]).astype(o_ref.dtype)


def _modulate(x, modr, sc_row, sh_row):
    B, S, D = x.shape
    ts = min(512, S)
    nst = S // ts
    return pl.pallas_call(
        _modulate_kernel,
        out_shape=jax.ShapeDtypeStruct((B * S, D), BF16),
        grid=(B, nst),
        in_specs=[
            pl.BlockSpec((None, ts, D), lambda b, s: (b, s, 0)),
            pl.BlockSpec((None, 1, D), lambda b, s: (sc_row(b), 0, 0)),
            pl.BlockSpec((None, 1, D), lambda b, s: (sh_row(b), 0, 0)),
        ],
        out_specs=pl.BlockSpec((ts, D), lambda b, s: (b * nst + s, 0)),
        compiler_params=_cparams(("parallel", "parallel"), 3 * _nbytes((ts, D), F32) + 2**22),
        name="modulate",
    )(x, modr, modr)


M_SPLIT = 2


def _row_chunks(tm):
    rows = tm // M_SPLIT
    return [slice(c * rows, (c + 1) * rows) for c in range(M_SPLIT)]


def _cast_weights_once(w_ref, wb_ref):
    @pl.when(pl.program_id(1) == 0)
    def _():
        wb_ref[...] = w_ref[...].astype(wb_ref.dtype)


def _proj_ac_kernel(a_ref, w_ref, o_ref, wb_ref, *, n_gelu):
    j = pl.program_id(0)
    _cast_weights_once(w_ref, wb_ref)

    @pl.when(j < n_gelu)
    def _():
        for rs in _row_chunks(a_ref.shape[0]):
            acc = jnp.dot(a_ref[rs, :], wb_ref[...], preferred_element_type=F32)
            o_ref[rs, :] = jax.nn.gelu(acc).astype(o_ref.dtype)

    @pl.when(j >= n_gelu)
    def _():
        for rs in _row_chunks(a_ref.shape[0]):
            acc = jnp.dot(a_ref[rs, :], wb_ref[...], preferred_element_type=F32)
            o_ref[rs, :] = acc.astype(o_ref.dtype)


def _proj_vmem(tm, D, tn, out_dtype):
    return 2 * (_nbytes((tm, D), BF16) + _nbytes((D, tn), F32) + _nbytes((tm, tn), out_dtype)) \
        + _nbytes((D, tn), BF16) + 4 * _nbytes((tm, tn), F32)


def _proj_ac(h, w_in, layer, mix):
    T, D = h.shape
    tm = min(1024, T)
    tn = min(512, mix)
    n_gelu = 2 * mix // tn
    skip = 9 * mix // tn

    def w_col(j, i):
        return (layer, 0, jnp.where(j < n_gelu, j, j + skip))

    return pl.pallas_call(
        functools.partial(_proj_ac_kernel, n_gelu=n_gelu),
        out_shape=jax.ShapeDtypeStruct((T, 5 * mix), BF16),
        grid=(5 * mix // tn, T // tm),
        in_specs=[pl.BlockSpec((tm, D), lambda j, i: (i, 0)), pl.BlockSpec((None, D, tn), w_col)],
        out_specs=pl.BlockSpec((tm, tn), lambda j, i: (i, j)),
        scratch_shapes=[pltpu.VMEM((D, tn), BF16)],
        compiler_params=_cparams(("arbitrary", "arbitrary"), _proj_vmem(tm, D, tn, BF16)),
        name="proj_ac",
    )(h, w_in)


def _store_heads(o_ref, rs, val):
    for hh in range(val.shape[1] // HEAD_DIM):
        o_ref[hh, rs, :] = val[:, hh * HEAD_DIM:(hh + 1) * HEAD_DIM]


def _proj_qkv_kernel(a_ref, w_ref, cos_ref, sin_ref, o_ref, wb_ref, *, n_rope):
    j = pl.program_id(0)
    tn = wb_ref.shape[1]
    _cast_weights_once(w_ref, wb_ref)

    @pl.when(j < n_rope)
    def _():
        reps = tn // HEAD_DIM
        half = ROT_DIM // 2
        for rs in _row_chunks(a_ref.shape[0]):
            acc = jnp.dot(a_ref[rs, :], wb_ref[...], preferred_element_type=F32)
            cos = jnp.concatenate([cos_ref[rs, :]] * reps, axis=1)
            sin = jnp.concatenate([sin_ref[rs, :]] * reps, axis=1)
            lane = lax.broadcasted_iota(jnp.int32, acc.shape, 1) % HEAD_DIM
            partner = jnp.where(lane < half, pltpu.roll(acc, tn - half, 1), pltpu.roll(acc, half, 1))
            _store_heads(o_ref, rs, acc * cos + partner * sin)

    @pl.when(j >= n_rope)
    def _():
        for rs in _row_chunks(a_ref.shape[0]):
            _store_heads(o_ref, rs, jnp.dot(a_ref[rs, :], wb_ref[...], preferred_element_type=F32))


def _proj_qkv(h, w_in, layer, cos_t, sin_t, mix):
    T, D = h.shape
    tm = min(1024, T)
    tn = min(512, mix)
    hpt = tn // HEAD_DIM
    first = 2 * mix // tn
    tab = pl.BlockSpec((tm, HEAD_DIM), lambda j, i: (i, 0))
    return pl.pallas_call(
        functools.partial(_proj_qkv_kernel, n_rope=6 * mix // tn),
        out_shape=jax.ShapeDtypeStruct((9 * mix // HEAD_DIM, T, HEAD_DIM), F32),
        grid=(9 * mix // tn, T // tm),
        in_specs=[
            pl.BlockSpec((tm, D), lambda j, i: (i, 0)),
            pl.BlockSpec((None, D, tn), lambda j, i: (layer, 0, first + j)),
            tab, tab,
        ],
        out_specs=pl.BlockSpec((hpt, tm, HEAD_DIM), lambda j, i: (j, i, 0)),
        scratch_shapes=[pltpu.VMEM((D, tn), BF16)],
        compiler_params=_cparams(("arbitrary", "arbitrary"),
                                 _proj_vmem(tm, D, tn, F32) + 4 * _nbytes((tm, HEAD_DIM), F32)),
        name="proj_qkv",
    )(h, w_in, cos_t, sin_t)


def _gmlp_kernel(u_ref, v_ref, g_ref, b_ref, wsp_ref, bspt_ref, o_ref, *, groups):
    tg = u_ref.shape[0]
    vn = _layer_norm_rows(v_ref[...].astype(F32), g_ref[...], b_ref[...]).astype(BF16)
    t_idx = lax.broadcasted_iota(jnp.int32, (CHUNK, CHUNK), 0)
    s_idx = lax.broadcasted_iota(jnp.int32, (CHUNK, CHUNK), 1)
    causal = s_idx <= t_idx
    for g in range(groups):
        cols = slice(g * CHUNK, (g + 1) * CHUNK)
        w = jnp.where(causal, wsp_ref[g], 0.0).astype(BF16)
        bias = bspt_ref[:, g:g + 1]
        for c in range(tg // CHUNK):
            rows = slice(c * CHUNK, (c + 1) * CHUNK)
            mixed = jnp.dot(w, vn[rows, cols], preferred_element_type=F32) + bias
            o_ref[rows, cols] = (u_ref[rows, cols].astype(F32) * mixed).astype(o_ref.dtype)


def _gmlp(proj, ln_g, ln_b, w_sp, b_sp, mix):
    T = proj.shape[0]
    groups = w_sp.shape[0]
    tg = min(512, T)
    row = pl.BlockSpec((1, mix), lambda i: (0, 0))
    return pl.pallas_call(
        functools.partial(_gmlp_kernel, groups=groups),
        out_shape=jax.ShapeDtypeStruct((T, mix), BF16),
        grid=(T // tg,),
        in_specs=[
            pl.BlockSpec((tg, mix), lambda i: (i, 0)),
            pl.BlockSpec((tg, mix), lambda i: (i, 1)),
            row, row,
            pl.BlockSpec((groups, CHUNK, CHUNK), lambda i: (0, 0, 0)),
            pl.BlockSpec((CHUNK, groups), lambda i: (0, 0)),
        ],
        out_specs=pl.BlockSpec((tg, mix), lambda i: (i, 0)),
        compiler_params=_cparams(("parallel",), 10 * _nbytes((tg, mix), F32)),
        name="gmlp",
    )(proj, proj, ln_g.reshape(1, mix), ln_b.reshape(1, mix), w_sp, b_sp.T)


def _attn_kernel(q_ref, kc_ref, kp_ref, vc_ref, vp_ref, o_ref, lse_ref, *, dilation, periods_per_seq):
    has_prev = (pl.program_id(0) % periods_per_seq) > 0
    qi = lax.broadcasted_iota(jnp.int32, (ATTN_BLOCK, ATTN_BLOCK), 0)
    kj = lax.broadcasted_iota(jnp.int32, (ATTN_BLOCK, ATTN_BLOCK), 1)
    mask_c = kj <= qi
    mask_p = (kj >= qi) & has_prev
    scale = HEAD_DIM ** -0.5

    def one_class(r, carry):
        rows = pl.ds(r, ATTN_BLOCK, stride=dilation) if dilation > 1 else pl.ds(0, ATTN_BLOCK)
        for h in range(q_ref.shape[0]):
            q = q_ref[h, rows, :].astype(BF16)
            s_c = lax.dot_general(q, kc_ref[h, rows, :].astype(BF16), _NT, preferred_element_type=F32) * scale
            s_p = lax.dot_general(q, kp_ref[h, rows, :].astype(BF16), _NT, preferred_element_type=F32) * scale
            s_c = jnp.where(mask_c, s_c, MASKED)
            s_p = jnp.where(mask_p, s_p, MASKED)
            m = jnp.maximum(jnp.max(s_c, axis=-1, keepdims=True), jnp.max(s_p, axis=-1, keepdims=True))
            p_c = jnp.exp(s_c - m)
            p_p = jnp.exp(s_p - m)
            l = jnp.sum(p_c, axis=-1, keepdims=True) + jnp.sum(p_p, axis=-1, keepdims=True)
            o = jnp.dot(p_c.astype(BF16), vc_ref[h, rows, :].astype(BF16), preferred_element_type=F32)
            o = o + jnp.dot(p_p.astype(BF16), vp_ref[h, rows, :].astype(BF16), preferred_element_type=F32)
            o_ref[h, rows, :] = o / l
            lse_ref[h, rows, :] = jnp.broadcast_to(m + jnp.log(l), (ATTN_BLOCK, HEAD_DIM))
        return carry

    if dilation > 1:
        lax.fori_loop(0, dilation, one_class, 0)
    else:
        one_class(0, 0)


def _dilated_attention(qkv, group, dilation, seq, heads):
    T = qkv.shape[1]
    period = ATTN_BLOCK * dilation
    periods_per_seq = seq // period
    hps = heads
    while 14 * _nbytes((hps, period, HEAD_DIM), F32) > ATTN_VMEM_BUDGET and hps % 2 == 0:
        hps //= 2

    def cur(part):
        return pl.BlockSpec((hps, period, HEAD_DIM), lambda p, h: ((3 * part + group) * (heads // hps) + h, p, 0))

    def prev(part):
        return pl.BlockSpec((hps, period, HEAD_DIM),
                            lambda p, h: ((3 * part + group) * (heads // hps) + h, jnp.maximum(p - 1, 0), 0))

    out = pl.BlockSpec((hps, period, HEAD_DIM), lambda p, h: (h, p, 0))
    shape = jax.ShapeDtypeStruct((heads, T, HEAD_DIM), F32)
    return pl.pallas_call(
        functools.partial(_attn_kernel, dilation=dilation, periods_per_seq=periods_per_seq),
        out_shape=(shape, shape),
        grid=(T // period, heads // hps),
        in_specs=[cur(0), cur(1), prev(1), cur(2), prev(2)],
        out_specs=(out, out),
        compiler_params=_cparams(("parallel", "parallel"), 14 * _nbytes((hps, period, HEAD_DIM), F32)),
        name=f"dilated_attention_d{dilation}",
    )(qkv, qkv, qkv, qkv, qkv)


def _combine_kernel(o0, o1, o2, l0, l1, l2, out_ref):
    for h in range(o0.shape[0]):
        a, b, c = l0[h], l1[h], l2[h]
        m = jnp.maximum(jnp.maximum(a, b), c)
        ea, eb, ec = jnp.exp(a - m), jnp.exp(b - m), jnp.exp(c - m)
        num = ea * o0[h] + eb * o1[h] + ec * o2[h]
        out_ref[:, h * HEAD_DIM:(h + 1) * HEAD_DIM] = (num / (ea + eb + ec)).astype(out_ref.dtype)


def _combine_groups(outs, lses):
    heads, T, _ = outs[0].shape
    tm = min(512, T)
    blk = pl.BlockSpec((heads, tm, HEAD_DIM), lambda i: (0, i, 0))
    return pl.pallas_call(
        _combine_kernel,
        out_shape=jax.ShapeDtypeStruct((T, heads * HEAD_DIM), BF16),
        grid=(T // tm,),
        in_specs=[blk] * 6,
        out_specs=pl.BlockSpec((tm, heads * HEAD_DIM), lambda i: (i, 0)),
        compiler_params=_cparams(("parallel",), 14 * _nbytes((heads, tm, HEAD_DIM), F32)),
        name="attn_combine",
    )(*outs, *lses)


def _conv_kernel(gb_ref, gc_ref, xin_ref, gcp_ref, xinp_ref, cw_ref, o_ref, *, tiles_per_seq):
    first = (pl.program_id(0) % tiles_per_seq) == 0
    z = gc_ref[...].astype(F32) * xin_ref[...].astype(F32)
    zp = gcp_ref[...].astype(F32) * xinp_ref[...].astype(F32)
    zp = jnp.where(first, 0.0, zp)
    last = zp.shape[0] - 1
    rows = lax.broadcasted_iota(jnp.int32, z.shape, 0)
    z1 = jnp.where(rows == 0, zp[last:last + 1], pltpu.roll(z, 1, 0))
    z2 = jnp.where(rows == 0, zp[last - 1:last], jnp.where(rows == 1, zp[last:last + 1], pltpu.roll(z, 2, 0)))
    y = cw_ref[0:1, :] * z2 + cw_ref[1:2, :] * z1 + cw_ref[2:3, :] * z
    o_ref[...] = (gb_ref[...].astype(F32) * y).astype(o_ref.dtype)


def _short_conv(proj, conv_w, seq, mix, col0):
    T = proj.shape[0]
    tc = min(512, seq)
    tiles_per_seq = seq // tc
    halo = V7X_BF16_SUBLANES
    cb = col0 // mix
    per_halo = tc // halo

    def cur(k):
        return pl.BlockSpec((tc, mix), lambda i: (i, cb + k))

    def prev(k):
        return pl.BlockSpec((halo, mix), lambda i: (jnp.maximum(i * per_halo - 1, 0), cb + k))

    return pl.pallas_call(
        functools.partial(_conv_kernel, tiles_per_seq=tiles_per_seq),
        out_shape=jax.ShapeDtypeStruct((T, mix), BF16),
        grid=(T // tc,),
        in_specs=[cur(0), cur(1), cur(2), prev(1), prev(2),
                  pl.BlockSpec(conv_w.shape, lambda i: (0, 0))],
        out_specs=pl.BlockSpec((tc, mix), lambda i: (i, 0)),
        compiler_params=_cparams(("parallel",), 16 * _nbytes((tc, mix), F32)),
        name="short_conv",
    )(proj, proj, proj, proj, proj, conv_w)


def _merge_kernel(h_ref, ba_ref, bb_ref, bc_ref, wg_ref, bg_ref, wb_ref, o_ref, wgb_ref):
    _cast_weights_once(wg_ref, wgb_ref)
    h = h_ref[...]
    total = None
    for g, br in enumerate((ba_ref, bb_ref, bc_ref)):
        gate = jax.nn.sigmoid(jnp.dot(h, wgb_ref[g], preferred_element_type=F32) + bg_ref[g])
        term = gate * jnp.dot(br[...], wb_ref[g], preferred_element_type=F32)
        total = term if total is None else total + term
    o_ref[...] = total.astype(o_ref.dtype)


def _merge(h, branches, w_gate, layer, b_gate, w_branch):
    T, D = h.shape
    mix = branches[0].shape[1]
    tm = min(512, T)
    tn = min(256, D)
    br = pl.BlockSpec((tm, mix), lambda j, i: (i, 0))
    need = 2 * (_nbytes((tm, D), BF16) + 3 * _nbytes((tm, mix), BF16) + 3 * _nbytes((D, tn), F32)
                + 3 * _nbytes((mix, tn), BF16) + _nbytes((tm, tn), BF16)) \
        + 3 * _nbytes((D, tn), BF16) + 8 * _nbytes((tm, tn), F32)
    return pl.pallas_call(
        _merge_kernel,
        out_shape=jax.ShapeDtypeStruct((T, D), BF16),
        grid=(D // tn, T // tm),
        in_specs=[
            pl.BlockSpec((tm, D), lambda j, i: (i, 0)),
            br, br, br,
            pl.BlockSpec((None, N_BRANCH, D, tn), lambda j, i: (layer, 0, 0, j)),
            pl.BlockSpec((N_BRANCH, 1, tn), lambda j, i: (0, 0, j)),
            pl.BlockSpec((None, N_BRANCH, mix, tn), lambda j, i: (layer, 0, 0, j)),
        ],
        out_specs=pl.BlockSpec((tm, tn), lambda j, i: (i, j)),
        scratch_shapes=[pltpu.VMEM((N_BRANCH, D, tn), BF16)],
        compiler_params=_cparams(("arbitrary", "arbitrary"), need),
        name="branch_merge",
    )(h, *branches, w_gate, b_gate.reshape(N_BRANCH, 1, D), w_branch)


def _matmul_kernel(a_ref, w_ref, o_ref, wb_ref):
    _cast_weights_once(w_ref, wb_ref)
    o_ref[...] = jnp.dot(a_ref[...], wb_ref[...], preferred_element_type=F32).astype(o_ref.dtype)


def _matmul(a, w_stack, layer, out_dtype, name):
    T, K = a.shape
    N = w_stack.shape[2]
    tm = min(1024, T)
    tn = min(512, N)
    return pl.pallas_call(
        _matmul_kernel,
        out_shape=jax.ShapeDtypeStruct((T, N), out_dtype),
        grid=(N // tn, T // tm),
        in_specs=[pl.BlockSpec((tm, K), lambda j, i: (i, 0)),
                  pl.BlockSpec((None, K, tn), lambda j, i: (layer, 0, j))],
        out_specs=pl.BlockSpec((tm, tn), lambda j, i: (i, j)),
        scratch_shapes=[pltpu.VMEM((K, tn), BF16)],
        compiler_params=_cparams(("arbitrary", "arbitrary"), _proj_vmem(tm, K, tn, out_dtype)),
        name=name,
    )(a, w_stack)


def _residual_ln_kernel(x_ref, y_ref, gate_ref, lg_ref, lb_ref, *rest, alpha, y_transposed, emit_h, emit_ht):
    y = y_ref[...].astype(F32)
    if y_transposed:
        y = y.T
    xn = _layer_norm_rows(alpha * x_ref[...] + gate_ref[...] * y, lg_ref[...], lb_ref[...])
    if emit_h:
        sc_ref, sh_ref, xo_ref, ho_ref = rest[:4]
        h = xn * (1.0 + sc_ref[...]) + sh_ref[...]
        ho_ref[...] = h.astype(ho_ref.dtype)
        if emit_ht:
            rest[4][...] = h.T.astype(ho_ref.dtype)
    else:
        (xo_ref,) = rest
    xo_ref[...] = xn


def _residual_ln(x, y, modr, gate_row, ln_g, ln_b, alpha, *, y_transposed, next_rows=None, emit_ht=False):
    B, S, D = x.shape
    ts = min(256, S)
    nst = S // ts
    emit_h = next_rows is not None

    def mod_spec(row_fn):
        return pl.BlockSpec((None, 1, D), lambda b, s: (row_fn(b), 0, 0))

    vec = pl.BlockSpec((1, D), lambda b, s: (0, 0))
    if y_transposed:
        y_spec = pl.BlockSpec((D, ts), lambda b, s: (0, b * nst + s))
    else:
        y_spec = pl.BlockSpec((ts, D), lambda b, s: (b * nst + s, 0))
    x_spec = pl.BlockSpec((None, ts, D), lambda b, s: (b, s, 0))
    in_specs = [x_spec, y_spec, mod_spec(gate_row), vec, vec]
    args = [x, y, modr, ln_g.reshape(1, D), ln_b.reshape(1, D)]
    out_shape = [jax.ShapeDtypeStruct((B, S, D), F32)]
    out_specs = [x_spec]
    if emit_h:
        in_specs += [mod_spec(next_rows[0]), mod_spec(next_rows[1])]
        args += [modr, modr]
        out_shape.append(jax.ShapeDtypeStruct((B * S, D), BF16))
        out_specs.append(pl.BlockSpec((ts, D), lambda b, s: (b * nst + s, 0)))
        if emit_ht:
            out_shape.append(jax.ShapeDtypeStruct((D, B * S), BF16))
            out_specs.append(pl.BlockSpec((D, ts), lambda b, s: (0, b * nst + s)))
    res = pl.pallas_call(
        functools.partial(_residual_ln_kernel, alpha=alpha, y_transposed=y_transposed, emit_h=emit_h,
                          emit_ht=emit_ht),
        out_shape=tuple(out_shape),
        grid=(B, nst),
        in_specs=in_specs,
        out_specs=tuple(out_specs),
        compiler_params=_cparams(("parallel", "parallel"), 14 * _nbytes((ts, D), F32)),
        name="residual_ln",
    )(*args)
    return tuple(res) + (None,) * (3 - len(res))


def _peer_scores_kernel(h_ref, wpq_ref, keys_ref, st_ref):
    q = jnp.dot(h_ref[...], wpq_ref[...], preferred_element_type=F32).astype(BF16)
    st_ref[...] = lax.dot_general(keys_ref[...], q, _NT, preferred_element_type=F32)


def _peer_scores(h, w_pq, keys_bd):
    T, D = h.shape
    HQ = w_pq.shape[1]
    R = keys_bd.shape[0]
    tm = min(512, T)
    need = 2 * (_nbytes((tm, D), BF16) + _nbytes((D, HQ), BF16) + _nbytes((R, HQ), BF16)
                + _nbytes((R, tm), F32)) + 2 * _nbytes((tm, HQ), F32)
    return pl.pallas_call(
        _peer_scores_kernel,
        out_shape=jax.ShapeDtypeStruct((R, T), F32),
        grid=(T // tm,),
        in_specs=[
            pl.BlockSpec((tm, D), lambda i: (i, 0)),
            pl.BlockSpec((D, HQ), lambda i: (0, 0)),
            pl.BlockSpec((R, HQ), lambda i: (0, 0)),
        ],
        out_specs=pl.BlockSpec((R, tm), lambda i: (0, i)),
        compiler_params=_cparams(("parallel",), need),
        name="peer_scores",
    )(h, w_pq, keys_bd)


def _top16_ranks(s, row_ids, exact):
    n = s.shape[0]
    k_ids = lax.broadcasted_iota(jnp.int32, (PEER_TOPK, s.shape[1]), 0)
    rank = jnp.full(s.shape, float(PEER_TOPK), F32)
    tops = jnp.zeros((PEER_TOPK, s.shape[1]), F32)
    for k in range(PEER_TOPK):
        m = jnp.max(s, axis=0, keepdims=True)
        sel = s == m
        if exact:
            first = jnp.min(jnp.where(sel, row_ids, float(n)), axis=0, keepdims=True)
            sel = row_ids == first
        rank = jnp.where(sel, float(k), rank)
        s = jnp.where(sel, -jnp.inf, s)
        tops = jnp.where(k_ids == k, m, tops)
    count = jnp.sum(jnp.where(rank < float(PEER_TOPK), 1.0, 0.0), axis=0, keepdims=True)
    return rank, tops, count


def _pair_counts(t1, t2, exact):
    L = t1.shape[1]
    K = PEER_TOPK
    sub = 8
    pieces = [t1[0:1] + t2]
    pos = [lax.broadcasted_iota(jnp.int32, (K, L), 0).astype(F32)]
    r8 = lax.broadcasted_iota(jnp.int32, (sub, L), 0).astype(F32)
    for a in range(1, sub):
        pieces.append(t1[a:a + 1] + t2[0:sub])
        pos.append(r8 + float(a * K))
    pieces.append(t1[sub:K] + t2[0:1])
    pos.append((r8 + float(sub)) * float(K))
    v = jnp.concatenate(pieces, axis=0)
    p = jnp.concatenate(pos, axis=0)
    picked = jnp.zeros(v.shape, F32)
    z = jnp.zeros((1, L), F32)
    top = None
    for k in range(K):
        m = jnp.max(v, axis=0, keepdims=True)
        sel = v == m
        if exact:
            first = jnp.min(jnp.where(sel, p, float(K * K)), axis=0, keepdims=True)
            sel = p == first
        picked = jnp.where(sel, 1.0, picked)
        v = jnp.where(sel, -jnp.inf, v)
        if k == 0:
            top = m
            z = z + 1.0
        else:
            z = z + jnp.exp(m - top)
    a_ids = lax.broadcasted_iota(jnp.int32, (sub, L), 0)
    low = jnp.zeros((sub, L), F32)
    low = jnp.where(a_ids == 0, jnp.sum(picked[0:K], axis=0, keepdims=True), low)
    for a in range(1, sub):
        off = K + (a - 1) * sub
        low = jnp.where(a_ids == a, jnp.sum(picked[off:off + sub], axis=0, keepdims=True), low)
    counts = jnp.concatenate([low, picked[K + (sub - 1) * sub:]], axis=0)
    return counts, z, jnp.sum(picked, axis=0, keepdims=True)


def _peer_select_kernel(st_ref, rank2_ref, cnt1_ref, e1_ref, e2_ref, *, heads, nk):
    tt = st_ref.shape[1]
    row_ids = lax.broadcasted_iota(jnp.int32, (nk, V7X_LANES), 0).astype(F32)
    full = float(PEER_TOPK)

    hpi = SELECT_HEADS_PER_ITER if heads % SELECT_HEADS_PER_ITER == 0 else 1

    def per_iter(idx, carry):
        c = idx // (heads // hpi)
        lanes = pl.ds(pl.multiple_of(c * V7X_LANES, V7X_LANES), V7X_LANES)

        def select(exact):
            flags = []
            for k in range(hpi):
                hh = (idx % (heads // hpi)) * hpi + k
                r1 = pl.ds(pl.multiple_of((2 * hh) * nk, nk), nk)
                r2 = pl.ds(pl.multiple_of((2 * hh + 1) * nk, nk), nk)
                ro = pl.ds(pl.multiple_of(hh * nk, nk), nk)
                s1 = st_ref[r1, lanes]
                s2 = st_ref[r2, lanes]
                rank1, t1, n1 = _top16_ranks(s1, row_ids, exact)
                rank2, t2, n2 = _top16_ranks(s2, row_ids, exact)
                counts, z, n3 = _pair_counts(t1, t2, exact)
                cnt1 = jnp.zeros((nk, V7X_LANES), F32)
                for a in range(PEER_TOPK):
                    cnt1 = jnp.where(rank1 == float(a), counts[a:a + 1], cnt1)
                rank2_ref[ro, lanes] = rank2.astype(rank2_ref.dtype)
                cnt1_ref[ro, lanes] = cnt1
                e1_ref[ro, lanes] = jnp.exp(s1 - t1[0:1]) / z
                e2_ref[ro, lanes] = jnp.exp(s2 - t2[0:1]).astype(e2_ref.dtype)
                flags.append(jnp.where((n1 == full) & (n2 == full) & (n3 == full), 0.0, 1.0))
            return functools.reduce(jnp.maximum, flags)

        tied = jnp.max(select(exact=False)) > 0.0

        @pl.when(tied)
        def _():
            select(exact=True)

        return carry

    lax.fori_loop(0, (tt // V7X_LANES) * (heads // hpi), per_iter, 0)


def _peer_select(st, heads, nk):
    R, T = st.shape
    tt = min(512, T)
    blk = pl.BlockSpec((heads * nk, tt), lambda i: (0, i))
    return pl.pallas_call(
        functools.partial(_peer_select_kernel, heads=heads, nk=nk),
        out_shape=tuple(jax.ShapeDtypeStruct((heads * nk, T), dt) for dt in (BF16, F32, F32, BF16)),
        grid=(T // tt,),
        in_specs=[pl.BlockSpec((R, tt), lambda i: (0, i))],
        out_specs=(blk,) * 4,
        compiler_params=_cparams(("parallel",), 2 * _nbytes((R, tt), F32) + 8 * _nbytes((heads * nk, tt), F32)),
        name="peer_select",
    )(st)


def _peer_dense_kernel(ht_ref, wu_ref, wv_ref, rank2_ref, cnt1_ref, e1_ref, e2_ref, yt_ref,
                       a_ref, carry_ref, *, heads, nk, pieces, n_tiles):
    e = pl.program_id(1)

    @pl.when(e == 0)
    def _():
        yt_ref[...] = jnp.zeros_like(yt_ref)
        carry_ref[...] = jnp.zeros_like(carry_ref)

    keys_per_step = sum(pieces)
    tile = jnp.minimum(e, n_tiles - 1)
    ht = ht_ref[...]
    tm = ht.shape[1]
    pk = V7X_BF16_SUBLANES
    zero = jnp.zeros((), BF16)
    key0 = [sum(pieces[:p]) for p in range(len(pieces))]
    acts = [jnp.dot(wu_ref[key0[p] * nk:(key0[p] + pieces[p]) * nk, :], ht, preferred_element_type=F32)
            for p in range(len(pieces))]

    def gated(p, dst_ref, row0):
        for b in range(pieces[p]):
            i1 = tile * keys_per_step + key0[p] + b
            gate = None
            for hh in range(heads):
                row = pl.ds(hh * nk + i1, 1)
                blk = slice(hh * nk, (hh + 1) * nk)
                cnt = jnp.broadcast_to(cnt1_ref[row, :], (pk, tm)).astype(BF16)[None]
                e1 = jnp.broadcast_to(e1_ref[row, :], (pk, tm)).astype(BF16)[None]
                r2 = rank2_ref[blk, :].reshape(nk // pk, pk, tm)
                e2 = e2_ref[blk, :].reshape(nk // pk, pk, tm)
                term = jnp.where(r2 < cnt, e2 * e1, zero)
                gate = term if gate is None else gate + term
            src = slice(b * nk, (b + 1) * nk)
            dst = slice(row0 + b * nk, row0 + (b + 1) * nk)
            dst_ref[dst, :] = jax.nn.gelu(acts[p][src, :]).astype(BF16) * gate.reshape(nk, tm)

    carried = pieces[-1] * nk
    a_ref[0:carried, :] = carry_ref[...]
    for p in range(len(pieces) - 1):
        gated(p, a_ref, carried + key0[p] * nk)
    yt_ref[...] += jnp.dot(wv_ref[...], a_ref[...], preferred_element_type=F32)
    gated(len(pieces) - 1, carry_ref, 0)


def _peer_pieces(keys_per_step):
    if keys_per_step >= 4:
        return (keys_per_step - 2, 1, 1)
    return (keys_per_step - 1, 1)


def _peer_dense(ht, w_u, layer, w_vt_padded, sel, heads, nk):
    D, T = ht.shape
    E = w_u.shape[1]
    tm = min(512, T)
    te = _peer_tile(nk)
    pieces = _peer_pieces(te // nk)
    n_tiles = E // te
    sel_spec = pl.BlockSpec((heads * nk, tm), lambda i, e: (0, i), pipeline_mode=pl.Buffered(1))
    need = _nbytes((tm, D), BF16) + 4 * _nbytes((te, D), BF16) + 2 * _nbytes((D, tm), F32) \
        + 4 * _nbytes((heads * nk, tm), F32) + 2 * _nbytes((te, tm), BF16) + 4 * _nbytes((te, tm), F32)
    return pl.pallas_call(
        functools.partial(_peer_dense_kernel, heads=heads, nk=nk, pieces=pieces, n_tiles=n_tiles),
        out_shape=jax.ShapeDtypeStruct((D, T), F32),
        grid=(T // tm, n_tiles + 1),
        in_specs=[
            pl.BlockSpec((D, tm), lambda i, e: (0, i), pipeline_mode=pl.Buffered(1)),
            pl.BlockSpec((None, te, D), lambda i, e: (layer, jnp.minimum(e, n_tiles - 1), 0)),
            pl.BlockSpec((D, te), lambda i, e: (0, e)),
            sel_spec, sel_spec, sel_spec, sel_spec,
        ],
        out_specs=pl.BlockSpec((D, tm), lambda i, e: (0, i)),
        scratch_shapes=[pltpu.VMEM((te, tm), BF16), pltpu.VMEM((pieces[-1] * nk, tm), BF16)],
        compiler_params=_cparams(("parallel", "arbitrary"), need),
        name="peer_dense",
    )(ht, w_u, w_vt_padded, *sel)


def _peer_tile(nk):
    return min(4, nk) * nk


def _transpose_pad_kernel(w_ref, o_ref, *, n_blocks):
    j = pl.program_id(0)
    inside = (j > 0) & (j <= n_blocks)

    @pl.when(inside)
    def _():
        o_ref[...] = w_ref[...].T.astype(o_ref.dtype)

    @pl.when(jnp.logical_not(inside))
    def _():
        o_ref[...] = jnp.zeros_like(o_ref)


def _transpose_pad(w_v, layer, front, back):
    _, E, D = w_v.shape
    n_blocks = E // front
    assert back % front == 0
    return pl.pallas_call(
        functools.partial(_transpose_pad_kernel, n_blocks=n_blocks),
        out_shape=jax.ShapeDtypeStruct((D, front + E + back), BF16),
        grid=(1 + n_blocks + back // front,),
        in_specs=[pl.BlockSpec((None, front, D), lambda j: (layer, jnp.clip(j - 1, 0, n_blocks - 1), 0))],
        out_specs=pl.BlockSpec((D, front), lambda j: (0, j)),
        compiler_params=_cparams(("parallel",), 6 * _nbytes((front, D), F32)),
        name="transpose_pad_wv",
    )(w_v)


def _mixer_sublayer(h, w_in, layer, cos_t, sin_t, w_gate, b_gate, ln_v_g, ln_v_b, w_sp, b_sp, conv_w,
                    w_branch, w_o, seq):
    mix = ln_v_g.shape[0]
    heads = mix // HEAD_DIM
    proj_ac = _proj_ac(h, w_in, layer, mix)
    qkv = _proj_qkv(h, w_in, layer, cos_t, sin_t, mix)
    branch_a = _gmlp(proj_ac, ln_v_g, ln_v_b, w_sp, b_sp, mix)
    outs, lses = [], []
    for g, (_, dilation) in enumerate(B_PATTERNS):
        o, lse = _dilated_attention(qkv, g, dilation, seq, heads)
        outs.append(o)
        lses.append(lse)
    branch_b = _combine_groups(outs, lses)
    branch_c = _short_conv(proj_ac, conv_w, seq, mix, 2 * mix)
    merged = _merge(h, (branch_a, branch_b, branch_c), w_gate, layer, b_gate, w_branch)
    return _matmul(merged, w_o, layer, BF16, "out_proj")


def _peer_sublayer(h, ht, w_pq, sub_keys, w_u, layer, w_v):
    heads, _, nk, _ = sub_keys.shape
    keys_bd = jax.scipy.linalg.block_diag(
        *[sub_keys[hh, p] for hh in range(heads) for p in range(2)]).astype(BF16)
    te = _peer_tile(nk)
    carried = _peer_pieces(te // nk)[-1] * nk
    w_vt_padded = _transpose_pad(w_v, layer, carried, te - carried)
    st = _peer_scores(h, w_pq.astype(BF16), keys_bd)
    sel = _peer_select(st, heads, nk)
    return _peer_dense(ht, w_u, layer, w_vt_padded, sel, heads, nk)


def kernel(x, c, positions, w_ada, b_ada, w_in, w_gate, b_gate, ln_v_g, ln_v_b, w_sp, b_sp,
           conv_w, w_branch, w_o, ln1_g, ln1_b, w_pq, sub_keys, w_u, w_v, ln2_g, ln2_b):
    B, S, D = x.shape
    depth = w_ada.shape[0]
    alpha = (2 * depth) ** 0.25
    assert S % (ATTN_BLOCK * B_PATTERNS[-1][1]) == 0

    cos_t, sin_t = _rope_tables(positions)
    modr = _adaln(c, w_ada, b_ada)

    def mod_row(layer, k):
        return lambda b: (layer * B + b) * 6 + k

    w_u_b = w_u.astype(BF16)
    w_branch_b = w_branch.astype(BF16)

    h = _modulate(x, modr, mod_row(0, 1), mod_row(0, 0))
    for l in range(depth):
        y = _mixer_sublayer(h, w_in, l, cos_t, sin_t, w_gate, b_gate[l],
                            ln_v_g[l], ln_v_b[l], w_sp[l], b_sp[l], conv_w[l],
                            w_branch_b, w_o, S)
        x, h, ht = _residual_ln(x, y, modr, mod_row(l, 2), ln1_g[l], ln1_b[l], alpha, y_transposed=False,
                                next_rows=(mod_row(l, 4), mod_row(l, 3)), emit_ht=True)
        y_t = _peer_sublayer(h, ht, w_pq[l], sub_keys[l], w_u_b, l, w_v)
        nxt = (mod_row(l + 1, 1), mod_row(l + 1, 0)) if l + 1 < depth else None
        x, h, _ = _residual_ln(x, y_t, modr, mod_row(l, 5), ln2_g[l], ln2_b[l], alpha, y_transposed=True,
                               next_rows=nxt)
    return x
```

```python
import functools

import jax
import jax.numpy as jnp
from jax import lax
from jax.experimental import pallas as pl
from jax.experimental.pallas import tpu as pltpu

F32 = jnp.float32
BF16 = jnp.bfloat16

HEAD_DIM = 128
CHUNK = 128
ATTN_BLOCK = 128
ROT_DIM = HEAD_DIM // 4
ROPE_THETA = 500000.0
B_PATTERNS = ((128, 1), (512, 4), (2048, 16))
N_BRANCH = 3
PEER_TOPK = 16
LN_EPS = 1e-5
MASKED = -1e30

V7X_LANES = 128
V7X_BF16_SUBLANES = 16
V7X_VMEM_LIMIT_CAP = 58 * 2**20
SPILL_AND_TEMP_BYTES = 4 * 2**20
ATTN_VMEM_BUDGET = 32 * 2**20
SELECT_HEADS_PER_ITER = 8

def _cparams(semantics, vmem_bytes, flags=None):
    limit = max(vmem_bytes + SPILL_AND_TEMP_BYTES, 16 * 2**20)
    return pltpu.CompilerParams(
        dimension_semantics=semantics,
        vmem_limit_bytes=int(min(limit, V7X_VMEM_LIMIT_CAP)),
        flags=flags,
    )


def _nbytes(shape, dtype):
    n = 1
    for s in shape:
        n *= s
    return n * jnp.dtype(dtype).itemsize


def _layer_norm_rows(z, g, b):
    mu = jnp.mean(z, axis=-1, keepdims=True)
    zc = z - mu
    var = jnp.mean(zc * zc, axis=-1, keepdims=True)
    return zc * lax.rsqrt(var + LN_EPS) * g + b


_NT = (((1,), (1,)), ((), ()))


def _rope_table_kernel(pos_ref, freq_ref, sign_ref, cos_ref, sin_ref):
    ang = pos_ref[...].astype(F32) * freq_ref[...]
    cos_ref[...] = jnp.cos(ang)
    sin_ref[...] = jnp.sin(ang) * sign_ref[...]


def _rope_tables(positions):
    T = positions.size
    half = ROT_DIM // 2
    inv_freq = ROPE_THETA ** (-jnp.arange(half, dtype=F32) / half)
    zeros = jnp.zeros((HEAD_DIM - ROT_DIM,), F32)
    freq = jnp.concatenate([inv_freq, inv_freq, zeros])[None, :]
    sign = jnp.concatenate([-jnp.ones((half,), F32), jnp.ones((half,), F32), zeros])[None, :]
    tm = min(1024, T)
    row = pl.BlockSpec((1, HEAD_DIM), lambda i: (0, 0))
    tab = pl.BlockSpec((tm, HEAD_DIM), lambda i: (i, 0))
    return pl.pallas_call(
        _rope_table_kernel,
        out_shape=(jax.ShapeDtypeStruct((T, HEAD_DIM), F32),) * 2,
        grid=(T // tm,),
        in_specs=[pl.BlockSpec((tm, 1), lambda i: (i, 0)), row, row],
        out_specs=(tab, tab),
        compiler_params=_cparams(("parallel",), 8 * _nbytes((tm, HEAD_DIM), F32)),
        name="rope_table",
    )(positions.reshape(T, 1), freq, sign)


def _adaln_kernel(c_ref, w_ref, b_ref, o_ref):
    c = c_ref[...]
    o_ref[...] = jnp.dot(c * jax.nn.sigmoid(c), w_ref[...], preferred_element_type=F32) + b_ref[...]


def _adaln(c, w_ada, b_ada):
    L, D, N = w_ada.shape
    B = c.shape[0]
    rows = 8
    c_pad = jnp.zeros((rows, D), F32).at[:B].set(c)
    tn = min(512, N)
    mod = pl.pallas_call(
        _adaln_kernel,
        out_shape=jax.ShapeDtypeStruct((L, rows, N), F32),
        grid=(L, N // tn),
        in_specs=[
            pl.BlockSpec((rows, D), lambda l, j: (0, 0)),
            pl.BlockSpec((None, D, tn), lambda l, j: (l, 0, j)),
            pl.BlockSpec((None, 1, tn), lambda l, j: (l, 0, j)),
        ],
        out_specs=pl.BlockSpec((None, rows, tn), lambda l, j: (l, 0, j)),
        compiler_params=_cparams(("parallel", "parallel"), 2 * _nbytes((D, tn), F32) + 2**22),
        name="adaln",
    )(c_pad, w_ada, b_ada.reshape(L, 1, N))
    return mod[:, :B].reshape(L * B * 6, 1, D)


def _modulate_kernel(x_ref, sc_ref, sh_ref, o_ref):
    o_ref[...] = (x_ref[...] * (1.0 + sc_ref[...]) + sh_ref[...]).astype(o_ref.dtype)


def _modulate(x, modr, sc_row, sh_row):
    B, S, D = x.shape
    ts = min(512, S)
    nst = S // ts
    return pl.pallas_call(
        _modulate_kernel,
        out_shape=jax.ShapeDtypeStruct((B * S, D), BF16),
        grid=(B, nst),
        in_specs=[
            pl.BlockSpec((None, ts, D), lambda b, s: (b, s, 0)),
            pl.BlockSpec((None, 1, D), lambda b, s: (sc_row(b), 0, 0)),
            pl.BlockSpec((None, 1, D), lambda b, s: (sh_row(b), 0, 0)),
        ],
        out_specs=pl.BlockSpec((ts, D), lambda b, s: (b * nst + s, 0)),
        compiler_params=_cparams(("parallel", "parallel"), 3 * _nbytes((ts, D), F32) + 2**22),
        name="modulate",
    )(x, modr, modr)


M_SPLIT = 2


def _row_chunks(tm):
    rows = tm // M_SPLIT
    return [slice(c * rows, (c + 1) * rows) for c in range(M_SPLIT)]


def _cast_weights_once(w_ref, wb_ref):
    @pl.when(pl.program_id(1) == 0)
    def _():
        wb_ref[...] = w_ref[...].astype(wb_ref.dtype)


def _proj_ac_kernel(a_ref, w_ref, o_ref, wb_ref, *, n_gelu):
    j = pl.program_id(0)
    _cast_weights_once(w_ref, wb_ref)

    @pl.when(j < n_gelu)
    def _():
        for rs in _row_chunks(a_ref.shape[0]):
            acc = jnp.dot(a_ref[rs, :], wb_ref[...], preferred_element_type=F32)
            o_ref[rs, :] = jax.nn.gelu(acc).astype(o_ref.dtype)

    @pl.when(j >= n_gelu)
    def _():
        for rs in _row_chunks(a_ref.shape[0]):
            acc = jnp.dot(a_ref[rs, :], wb_ref[...], preferred_element_type=F32)
            o_ref[rs, :] = acc.astype(o_ref.dtype)


def _proj_vmem(tm, D, tn, out_dtype):
    return 2 * (_nbytes((tm, D), BF16) + _nbytes((D, tn), F32) + _nbytes((tm, tn), out_dtype)) \
        + _nbytes((D, tn), BF16) + 4 * _nbytes((tm, tn), F32)


def _proj_ac(h, w_in, layer, mix):
    T, D = h.shape
    tm = min(1024, T)
    tn = min(512, mix)
    n_gelu = 2 * mix // tn
    skip = 9 * mix // tn

    def w_col(j, i):
        return (layer, 0, jnp.where(j < n_gelu, j, j + skip))

    return pl.pallas_call(
        functools.partial(_proj_ac_kernel, n_gelu=n_gelu),
        out_shape=jax.ShapeDtypeStruct((T, 5 * mix), BF16),
        grid=(5 * mix // tn, T // tm),
        in_specs=[pl.BlockSpec((tm, D), lambda j, i: (i, 0)), pl.BlockSpec((None, D, tn), w_col)],
        out_specs=pl.BlockSpec((tm, tn), lambda j, i: (i, j)),
        scratch_shapes=[pltpu.VMEM((D, tn), BF16)],
        compiler_params=_cparams(("arbitrary", "arbitrary"), _proj_vmem(tm, D, tn, BF16)),
        name="proj_ac",
    )(h, w_in)


def _store_heads(o_ref, rs, val):
    for hh in range(val.shape[1] // HEAD_DIM):
        o_ref[hh, rs, :] = val[:, hh * HEAD_DIM:(hh + 1) * HEAD_DIM]


def _proj_qkv_kernel(a_ref, w_ref, cos_ref, sin_ref, o_ref, wb_ref, *, n_rope):
    j = pl.program_id(0)
    tn = wb_ref.shape[1]
    _cast_weights_once(w_ref, wb_ref)

    @pl.when(j < n_rope)
    def _():
        reps = tn // HEAD_DIM
        half = ROT_DIM // 2
        for rs in _row_chunks(a_ref.shape[0]):
            acc = jnp.dot(a_ref[rs, :], wb_ref[...], preferred_element_type=F32)
            cos = jnp.concatenate([cos_ref[rs, :]] * reps, axis=1)
            sin = jnp.concatenate([sin_ref[rs, :]] * reps, axis=1)
            lane = lax.broadcasted_iota(jnp.int32, acc.shape, 1) % HEAD_DIM
            partner = jnp.where(lane < half, pltpu.roll(acc, tn - half, 1), pltpu.roll(acc, half, 1))
            _store_heads(o_ref, rs, acc * cos + partner * sin)

    @pl.when(j >= n_rope)
    def _():
        for rs in _row_chunks(a_ref.shape[0]):
            _store_heads(o_ref, rs, jnp.dot(a_ref[rs, :], wb_ref[...], preferred_element_type=F32))


def _proj_qkv(h, w_in, layer, cos_t, sin_t, mix):
    T, D = h.shape
    tm = min(1024, T)
    tn = min(512, mix)
    hpt = tn // HEAD_DIM
    first = 2 * mix // tn
    tab = pl.BlockSpec((tm, HEAD_DIM), lambda j, i: (i, 0))
    return pl.pallas_call(
        functools.partial(_proj_qkv_kernel, n_rope=6 * mix // tn),
        out_shape=jax.ShapeDtypeStruct((9 * mix // HEAD_DIM, T, HEAD_DIM), F32),
        grid=(9 * mix // tn, T // tm),
        in_specs=[
            pl.BlockSpec((tm, D), lambda j, i: (i, 0)),
            pl.BlockSpec((None, D, tn), lambda j, i: (layer, 0, first + j)),
            tab, tab,
        ],
        out_specs=pl.BlockSpec((hpt, tm, HEAD_DIM), lambda j, i: (j, i, 0)),
        scratch_shapes=[pltpu.VMEM((D, tn), BF16)],
        compiler_params=_cparams(("arbitrary", "arbitrary"),
                                 _proj_vmem(tm, D, tn, F32) + 4 * _nbytes((tm, HEAD_DIM), F32)),
        name="proj_qkv",
    )(h, w_in, cos_t, sin_t)


def _gmlp_kernel(u_ref, v_ref, g_ref, b_ref, wsp_ref, bspt_ref, o_ref, *, groups):
    tg = u_ref.shape[0]
    vn = _layer_norm_rows(v_ref[...].astype(F32), g_ref[...], b_ref[...]).astype(BF16)
    t_idx = lax.broadcasted_iota(jnp.int32, (CHUNK, CHUNK), 0)
    s_idx = lax.broadcasted_iota(jnp.int32, (CHUNK, CHUNK), 1)
    causal = s_idx <= t_idx
    for g in range(groups):
        cols = slice(g * CHUNK, (g + 1) * CHUNK)
        w = jnp.where(causal, wsp_ref[g], 0.0).astype(BF16)
        bias = bspt_ref[:, g:g + 1]
        for c in range(tg // CHUNK):
            rows = slice(c * CHUNK, (c + 1) * CHUNK)
            mixed = jnp.dot(w, vn[rows, cols], preferred_element_type=F32) + bias
            o_ref[rows, cols] = (u_ref[rows, cols].astype(F32) * mixed).astype(o_ref.dtype)


def _gmlp(proj, ln_g, ln_b, w_sp, b_sp, mix):
    T = proj.shape[0]
    groups = w_sp.shape[0]
    tg = min(512, T)
    row = pl.BlockSpec((1, mix), lambda i: (0, 0))
    return pl.pallas_call(
        functools.partial(_gmlp_kernel, groups=groups),
        out_shape=jax.ShapeDtypeStruct((T, mix), BF16),
        grid=(T // tg,),
        in_specs=[
            pl.BlockSpec((tg, mix), lambda i: (i, 0)),
            pl.BlockSpec((tg, mix), lambda i: (i, 1)),
            row, row,
            pl.BlockSpec((groups, CHUNK, CHUNK), lambda i: (0, 0, 0)),
            pl.BlockSpec((CHUNK, groups), lambda i: (0, 0)),
        ],
        out_specs=pl.BlockSpec((tg, mix), lambda i: (i, 0)),
        compiler_params=_cparams(("parallel",), 10 * _nbytes((tg, mix), F32)),
        name="gmlp",
    )(proj, proj, ln_g.reshape(1, mix), ln_b.reshape(1, mix), w_sp, b_sp.T)


def _attn_kernel(q_ref, kc_ref, kp_ref, vc_ref, vp_ref, o_ref, lse_ref, *, dilation, periods_per_seq):
    has_prev = (pl.program_id(0) % periods_per_seq) > 0
    qi = lax.broadcasted_iota(jnp.int32, (ATTN_BLOCK, ATTN_BLOCK), 0)
    kj = lax.broadcasted_iota(jnp.int32, (ATTN_BLOCK, ATTN_BLOCK), 1)
    mask_c = kj <= qi
    mask_p = (kj >= qi) & has_prev
    scale = HEAD_DIM ** -0.5

    def one_class(r, carry):
        rows = pl.ds(r, ATTN_BLOCK, stride=dilation) if dilation > 1 else pl.ds(0, ATTN_BLOCK)
        for h in range(q_ref.shape[0]):
            q = q_ref[h, rows, :].astype(BF16)
            s_c = lax.dot_general(q, kc_ref[h, rows, :].astype(BF16), _NT, preferred_element_type=F32) * scale
            s_p = lax.dot_general(q, kp_ref[h, rows, :].astype(BF16), _NT, preferred_element_type=F32) * scale
            s_c = jnp.where(mask_c, s_c, MASKED)
            s_p = jnp.where(mask_p, s_p, MASKED)
            m = jnp.maximum(jnp.max(s_c, axis=-1, keepdims=True), jnp.max(s_p, axis=-1, keepdims=True))
            p_c = jnp.exp(s_c - m)
            p_p = jnp.exp(s_p - m)
            l = jnp.sum(p_c, axis=-1, keepdims=True) + jnp.sum(p_p, axis=-1, keepdims=True)
            o = jnp.dot(p_c.astype(BF16), vc_ref[h, rows, :].astype(BF16), preferred_element_type=F32)
            o = o + jnp.dot(p_p.astype(BF16), vp_ref[h, rows, :].astype(BF16), preferred_element_type=F32)
            o_ref[h, rows, :] = o / l
            lse_ref[h, rows, :] = jnp.broadcast_to(m + jnp.log(l), (ATTN_BLOCK, HEAD_DIM))
        return carry

    if dilation > 1:
        lax.fori_loop(0, dilation, one_class, 0)
    else:
        one_class(0, 0)


def _dilated_attention(qkv, group, dilation, seq, heads):
    T = qkv.shape[1]
    period = ATTN_BLOCK * dilation
    periods_per_seq = seq // period
    hps = heads
    while 14 * _nbytes((hps, period, HEAD_DIM), F32) > ATTN_VMEM_BUDGET and hps % 2 == 0:
        hps //= 2

    def cur(part):
        return pl.BlockSpec((hps, period, HEAD_DIM), lambda p, h: ((3 * part + group) * (heads // hps) + h, p, 0))

    def prev(part):
        return pl.BlockSpec((hps, period, HEAD_DIM),
                            lambda p, h: ((3 * part + group) * (heads // hps) + h, jnp.maximum(p - 1, 0), 0))

    out = pl.BlockSpec((hps, period, HEAD_DIM), lambda p, h: (h, p, 0))
    shape = jax.ShapeDtypeStruct((heads, T, HEAD_DIM), F32)
    return pl.pallas_call(
        functools.partial(_attn_kernel, dilation=dilation, periods_per_seq=periods_per_seq),
        out_shape=(shape, shape),
        grid=(T // period, heads // hps),
        in_specs=[cur(0), cur(1), prev(1), cur(2), prev(2)],
        out_specs=(out, out),
        compiler_params=_cparams(("parallel", "parallel"), 14 * _nbytes((hps, period, HEAD_DIM), F32)),
        name=f"dilated_attention_d{dilation}",
    )(qkv, qkv, qkv, qkv, qkv)


def _combine_kernel(o0, o1, o2, l0, l1, l2, out_ref):
    for h in range(o0.shape[0]):
        a, b, c = l0[h], l1[h], l2[h]
        m = jnp.maximum(jnp.maximum(a, b), c)
        ea, eb, ec = jnp.exp(a - m), jnp.exp(b - m), jnp.exp(c - m)
        num = ea * o0[h] + eb * o1[h] + ec * o2[h]
        out_ref[:, h * HEAD_DIM:(h + 1) * HEAD_DIM] = (num / (ea + eb + ec)).astype(out_ref.dtype)


def _combine_groups(outs, lses):
    heads, T, _ = outs[0].shape
    tm = min(512, T)
    blk = pl.BlockSpec((heads, tm, HEAD_DIM), lambda i: (0, i, 0))
    return pl.pallas_call(
        _combine_kernel,
        out_shape=jax.ShapeDtypeStruct((T, heads * HEAD_DIM), BF16),
        grid=(T // tm,),
        in_specs=[blk] * 6,
        out_specs=pl.BlockSpec((tm, heads * HEAD_DIM), lambda i: (i, 0)),
        compiler_params=_cparams(("parallel",), 14 * _nbytes((heads, tm, HEAD_DIM), F32)),
        name="attn_combine",
    )(*outs, *lses)


def _conv_kernel(gb_ref, gc_ref, xin_ref, gcp_ref, xinp_ref, cw_ref, o_ref, *, tiles_per_seq):
    first = (pl.program_id(0) % tiles_per_seq) == 0
    z = gc_ref[...].astype(F32) * xin_ref[...].astype(F32)
    zp = gcp_ref[...].astype(F32) * xinp_ref[...].astype(F32)
    zp = jnp.where(first, 0.0, zp)
    last = zp.shape[0] - 1
    rows = lax.broadcasted_iota(jnp.int32, z.shape, 0)
    z1 = jnp.where(rows == 0, zp[last:last + 1], pltpu.roll(z, 1, 0))
    z2 = jnp.where(rows == 0, zp[last - 1:last], jnp.where(rows == 1, zp[last:last + 1], pltpu.roll(z, 2, 0)))
    y = cw_ref[0:1, :] * z2 + cw_ref[1:2, :] * z1 + cw_ref[2:3, :] * z
    o_ref[...] = (gb_ref[...].astype(F32) * y).astype(o_ref.dtype)


def _short_conv(proj, conv_w, seq, mix, col0):
    T = proj.shape[0]
    tc = min(512, seq)
    tiles_per_seq = seq // tc
    halo = V7X_BF16_SUBLANES
    cb = col0 // mix
    per_halo = tc // halo

    def cur(k):
        return pl.BlockSpec((tc, mix), lambda i: (i, cb + k))

    def prev(k):
        return pl.BlockSpec((halo, mix), lambda i: (jnp.maximum(i * per_halo - 1, 0), cb + k))

    return pl.pallas_call(
        functools.partial(_conv_kernel, tiles_per_seq=tiles_per_seq),
        out_shape=jax.ShapeDtypeStruct((T, mix), BF16),
        grid=(T // tc,),
        in_specs=[cur(0), cur(1), cur(2), prev(1), prev(2),
                  pl.BlockSpec(conv_w.shape, lambda i: (0, 0))],
        out_specs=pl.BlockSpec((tc, mix), lambda i: (i, 0)),
        compiler_params=_cparams(("parallel",), 16 * _nbytes((tc, mix), F32)),
        name="short_conv",
    )(proj, proj, proj, proj, proj, conv_w)


def _merge_kernel(h_ref, ba_ref, bb_ref, bc_ref, wg_ref, bg_ref, wb_ref, o_ref, wgb_ref):
    _cast_weights_once(wg_ref, wgb_ref)
    h = h_ref[...]
    total = None
    for g, br in enumerate((ba_ref, bb_ref, bc_ref)):
        gate = jax.nn.sigmoid(jnp.dot(h, wgb_ref[g], preferred_element_type=F32) + bg_ref[g])
        term = gate * jnp.dot(br[...], wb_ref[g], preferred_element_type=F32)
        total = term if total is None else total + term
    o_ref[...] = total.astype(o_ref.dtype)


def _merge(h, branches, w_gate, layer, b_gate, w_branch):
    T, D = h.shape
    mix = branches[0].shape[1]
    tm = min(512, T)
    tn = min(256, D)
    br = pl.BlockSpec((tm, mix), lambda j, i: (i, 0))
    need = 2 * (_nbytes((tm, D), BF16) + 3 * _nbytes((tm, mix), BF16) + 3 * _nbytes((D, tn), F32)
                + 3 * _nbytes((mix, tn), BF16) + _nbytes((tm, tn), BF16)) \
        + 3 * _nbytes((D, tn), BF16) + 8 * _nbytes((tm, tn), F32)
    return pl.pallas_call(
        _merge_kernel,
        out_shape=jax.ShapeDtypeStruct((T, D), BF16),
        grid=(D // tn, T // tm),
        in_specs=[
            pl.BlockSpec((tm, D), lambda j, i: (i, 0)),
            br, br, br,
            pl.BlockSpec((None, N_BRANCH, D, tn), lambda j, i: (layer, 0, 0, j)),
            pl.BlockSpec((N_BRANCH, 1, tn), lambda j, i: (0, 0, j)),
            pl.BlockSpec((None, N_BRANCH, mix, tn), lambda j, i: (layer, 0, 0, j)),
        ],
        out_specs=pl.BlockSpec((tm, tn), lambda j, i: (i, j)),
        scratch_shapes=[pltpu.VMEM((N_BRANCH, D, tn), BF16)],
        compiler_params=_cparams(("arbitrary", "arbitrary"), need),
        name="branch_merge",
    )(h, *branches, w_gate, b_gate.reshape(N_BRANCH, 1, D), w_branch)


def _matmul_kernel(a_ref, w_ref, o_ref, wb_ref):
    _cast_weights_once(w_ref, wb_ref)
    o_ref[...] = jnp.dot(a_ref[...], wb_ref[...], preferred_element_type=F32).astype(o_ref.dtype)


def _matmul(a, w_stack, layer, out_dtype, name):
    T, K = a.shape
    N = w_stack.shape[2]
    tm = min(1024, T)
    tn = min(512, N)
    return pl.pallas_call(
        _matmul_kernel,
        out_shape=jax.ShapeDtypeStruct((T, N), out_dtype),
        grid=(N // tn, T // tm),
        in_specs=[pl.BlockSpec((tm, K), lambda j, i: (i, 0)),
                  pl.BlockSpec((None, K, tn), lambda j, i: (layer, 0, j))],
        out_specs=pl.BlockSpec((tm, tn), lambda j, i: (i, j)),
        scratch_shapes=[pltpu.VMEM((K, tn), BF16)],
        compiler_params=_cparams(("arbitrary", "arbitrary"), _proj_vmem(tm, K, tn, out_dtype)),
        name=name,
    )(a, w_stack)


def _residual_ln_kernel(x_ref, y_ref, gate_ref, lg_ref, lb_ref, *rest, alpha, y_transposed, emit_h, emit_ht):
    y = y_ref[...].astype(F32)
    if y_transposed:
        y = y.T
    xn = _layer_norm_rows(alpha * x_ref[...] + gate_ref[...] * y, lg_ref[...], lb_ref[...])
    if emit_h:
        sc_ref, sh_ref, xo_ref, ho_ref = rest[:4]
        h = xn * (1.0 + sc_ref[...]) + sh_ref[...]
        ho_ref[...] = h.astype(ho_ref.dtype)
        if emit_ht:
            rest[4][...] = h.T.astype(ho_ref.dtype)
    else:
        (xo_ref,) = rest
    xo_ref[...] = xn


def _residual_ln(x, y, modr, gate_row, ln_g, ln_b, alpha, *, y_transposed, next_rows=None, emit_ht=False):
    B, S, D = x.shape
    ts = min(256, S)
    nst = S // ts
    emit_h = next_rows is not None

    def mod_spec(row_fn):
        return pl.BlockSpec((None, 1, D), lambda b, s: (row_fn(b), 0, 0))

    vec = pl.BlockSpec((1, D), lambda b, s: (0, 0))
    if y_transposed:
        y_spec = pl.BlockSpec((D, ts), lambda b, s: (0, b * nst + s))
    else:
        y_spec = pl.BlockSpec((ts, D), lambda b, s: (b * nst + s, 0))
    x_spec = pl.BlockSpec((None, ts, D), lambda b, s: (b, s, 0))
    in_specs = [x_spec, y_spec, mod_spec(gate_row), vec, vec]
    args = [x, y, modr, ln_g.reshape(1, D), ln_b.reshape(1, D)]
    out_shape = [jax.ShapeDtypeStruct((B, S, D), F32)]
    out_specs = [x_spec]
    if emit_h:
        in_specs += [mod_spec(next_rows[0]), mod_spec(next_rows[1])]
        args += [modr, modr]
        out_shape.append(jax.ShapeDtypeStruct((B * S, D), BF16))
        out_specs.append(pl.BlockSpec((ts, D), lambda b, s: (b * nst + s, 0)))
        if emit_ht:
            out_shape.append(jax.ShapeDtypeStruct((D, B * S), BF16))
            out_specs.append(pl.BlockSpec((D, ts), lambda b, s: (0, b * nst + s)))
    res = pl.pallas_call(
        functools.partial(_residual_ln_kernel, alpha=alpha, y_transposed=y_transposed, emit_h=emit_h,
                          emit_ht=emit_ht),
        out_shape=tuple(out_shape),
        grid=(B, nst),
        in_specs=in_specs,
        out_specs=tuple(out_specs),
        compiler_params=_cparams(("parallel", "parallel"), 14 * _nbytes((ts, D), F32)),
        name="residual_ln",
    )(*args)
    return tuple(res) + (None,) * (3 - len(res))


def _peer_scores_kernel(h_ref, wpq_ref, keys_ref, st_ref):
    q = jnp.dot(h_ref[...], wpq_ref[...], preferred_element_type=F32).astype(BF16)
    st_ref[...] = lax.dot_general(keys_ref[...], q, _NT, preferred_element_type=F32)


def _peer_scores(h, w_pq, keys_bd):
    T, D = h.shape
    HQ = w_pq.shape[1]
    R = keys_bd.shape[0]
    tm = min(512, T)
    need = 2 * (_nbytes((tm, D), BF16) + _nbytes((D, HQ), BF16) + _nbytes((R, HQ), BF16)
                + _nbytes((R, tm), F32)) + 2 * _nbytes((tm, HQ), F32)
    return pl.pallas_call(
        _peer_scores_kernel,
        out_shape=jax.ShapeDtypeStruct((R, T), F32),
        grid=(T // tm,),
        in_specs=[
            pl.BlockSpec((tm, D), lambda i: (i, 0)),
            pl.BlockSpec((D, HQ), lambda i: (0, 0)),
            pl.BlockSpec((R, HQ), lambda i: (0, 0)),
        ],
        out_specs=pl.BlockSpec((R, tm), lambda i: (0, i)),
        compiler_params=_cparams(("parallel",), need),
        name="peer_scores",
    )(h, w_pq, keys_bd)


def _top16_ranks(s, row_ids, exact):
    n = s.shape[0]
    k_ids = lax.broadcasted_iota(jnp.int32, (PEER_TOPK, s.shape[1]), 0)
    rank = jnp.full(s.shape, float(PEER_TOPK), F32)
    tops = jnp.zeros((PEER_TOPK, s.shape[1]), F32)
    for k in range(PEER_TOPK):
        m = jnp.max(s, axis=0, keepdims=True)
        sel = s == m
        if exact:
            first = jnp.min(jnp.where(sel, row_ids, float(n)), axis=0, keepdims=True)
            sel = row_ids == first
        rank = jnp.where(sel, float(k), rank)
        s = jnp.where(sel, -jnp.inf, s)
        tops = jnp.where(k_ids == k, m, tops)
    count = jnp.sum(jnp.where(rank < float(PEER_TOPK), 1.0, 0.0), axis=0, keepdims=True)
    return rank, tops, count


def _pair_counts(t1, t2, exact):
    L = t1.shape[1]
    K = PEER_TOPK
    sub = 8
    pieces = [t1[0:1] + t2]
    pos = [lax.broadcasted_iota(jnp.int32, (K, L), 0).astype(F32)]
    r8 = lax.broadcasted_iota(jnp.int32, (sub, L), 0).astype(F32)
    for a in range(1, sub):
        pieces.append(t1[a:a + 1] + t2[0:sub])
        pos.append(r8 + float(a * K))
    pieces.append(t1[sub:K] + t2[0:1])
    pos.append((r8 + float(sub)) * float(K))
    v = jnp.concatenate(pieces, axis=0)
    p = jnp.concatenate(pos, axis=0)
    picked = jnp.zeros(v.shape, F32)
    z = jnp.zeros((1, L), F32)
    top = None
    for k in range(K):
        m = jnp.max(v, axis=0, keepdims=True)
        sel = v == m
        if exact:
            first = jnp.min(jnp.where(sel, p, float(K * K)), axis=0, keepdims=True)
            sel = p == first
        picked = jnp.where(sel, 1.0, picked)
        v = jnp.where(sel, -jnp.inf, v)
        if k == 0:
            top = m
            z = z + 1.0
        else:
            z = z + jnp.exp(m - top)
    a_ids = lax.broadcasted_iota(jnp.int32, (sub, L), 0)
    low = jnp.zeros((sub, L), F32)
    low = jnp.where(a_ids == 0, jnp.sum(picked[0:K], axis=0, keepdims=True), low)
    for a in range(1, sub):
        off = K + (a - 1) * sub
        low = jnp.where(a_ids == a, jnp.sum(picked[off:off + sub], axis=0, keepdims=True), low)
    counts = jnp.concatenate([low, picked[K + (sub - 1) * sub:]], axis=0)
    return counts, z, jnp.sum(picked, axis=0, keepdims=True)


def _peer_select_kernel(st_ref, rank2_ref, cnt1_ref, e1_ref, e2_ref, *, heads, nk):
    tt = st_ref.shape[1]
    row_ids = lax.broadcasted_iota(jnp.int32, (nk, V7X_LANES), 0).astype(F32)
    full = float(PEER_TOPK)

    hpi = SELECT_HEADS_PER_ITER if heads % SELECT_HEADS_PER_ITER == 0 else 1

    def per_iter(idx, carry):
        c = idx // (heads // hpi)
        lanes = pl.ds(pl.multiple_of(c * V7X_LANES, V7X_LANES), V7X_LANES)

        def select(exact):
            flags = []
            for k in range(hpi):
                hh = (idx % (heads // hpi)) * hpi + k
                r1 = pl.ds(pl.multiple_of((2 * hh) * nk, nk), nk)
                r2 = pl.ds(pl.multiple_of((2 * hh + 1) * nk, nk), nk)
                ro = pl.ds(pl.multiple_of(hh * nk, nk), nk)
                s1 = st_ref[r1, lanes]
                s2 = st_ref[r2, lanes]
                rank1, t1, n1 = _top16_ranks(s1, row_ids, exact)
                rank2, t2, n2 = _top16_ranks(s2, row_ids, exact)
                counts, z, n3 = _pair_counts(t1, t2, exact)
                cnt1 = jnp.zeros((nk, V7X_LANES), F32)
                for a in range(PEER_TOPK):
                    cnt1 = jnp.where(rank1 == float(a), counts[a:a + 1], cnt1)
                rank2_ref[ro, lanes] = rank2.astype(rank2_ref.dtype)
                cnt1_ref[ro, lanes] = cnt1
                e1_ref[ro, lanes] = jnp.exp(s1 - t1[0:1]) / z
                e2_ref[ro, lanes] = jnp.exp(s2 - t2[0:1]).astype(e2_ref.dtype)
                flags.append(jnp.where((n1 == full) & (n2 == full) & (n3 == full), 0.0, 1.0))
            return functools.reduce(jnp.maximum, flags)

        tied = jnp.max(select(exact=False)) > 0.0

        @pl.when(tied)
        def _():
            select(exact=True)

        return carry

    lax.fori_loop(0, (tt // V7X_LANES) * (heads // hpi), per_iter, 0)


def _peer_select(st, heads, nk):
    R, T = st.shape
    tt = min(512, T)
    blk = pl.BlockSpec((heads * nk, tt), lambda i: (0, i))
    return pl.pallas_call(
        functools.partial(_peer_select_kernel, heads=heads, nk=nk),
        out_shape=tuple(jax.ShapeDtypeStruct((heads * nk, T), dt) for dt in (BF16, F32, F32, BF16)),
        grid=(T // tt,),
        in_specs=[pl.BlockSpec((R, tt), lambda i: (0, i))],
        out_specs=(blk,) * 4,
        compiler_params=_cparams(("parallel",), 2 * _nbytes((R, tt), F32) + 8 * _nbytes((heads * nk, tt), F32)),
        name="peer_select",
    )(st)


def _peer_dense_kernel(ht_ref, wu_ref, wv_ref, rank2_ref, cnt1_ref, e1_ref, e2_ref, yt_ref,
                       a_ref, carry_ref, *, heads, nk, pieces, n_tiles):
    e = pl.program_id(1)

    @pl.when(e == 0)
    def _():
        yt_ref[...] = jnp.zeros_like(yt_ref)
        carry_ref[...] = jnp.zeros_like(carry_ref)

    keys_per_step = sum(pieces)
    tile = jnp.minimum(e, n_tiles - 1)
    ht = ht_ref[...]
    tm = ht.shape[1]
    pk = V7X_BF16_SUBLANES
    zero = jnp.zeros((), BF16)
    key0 = [sum(pieces[:p]) for p in range(len(pieces))]
    acts = [jnp.dot(wu_ref[key0[p] * nk:(key0[p] + pieces[p]) * nk, :], ht, preferred_element_type=F32)
            for p in range(len(pieces))]

    def gated(p, dst_ref, row0):
        for b in range(pieces[p]):
            i1 = tile * keys_per_step + key0[p] + b
            gate = None
            for hh in range(heads):
                row = pl.ds(hh * nk + i1, 1)
                blk = slice(hh * nk, (hh + 1) * nk)
                cnt = jnp.broadcast_to(cnt1_ref[row, :], (pk, tm)).astype(BF16)[None]
                e1 = jnp.broadcast_to(e1_ref[row, :], (pk, tm)).astype(BF16)[None]
                r2 = rank2_ref[blk, :].reshape(nk // pk, pk, tm)
                e2 = e2_ref[blk, :].reshape(nk // pk, pk, tm)
                term = jnp.where(r2 < cnt, e2 * e1, zero)
                gate = term if gate is None else gate + term
            src = slice(b * nk, (b + 1) * nk)
            dst = slice(row0 + b * nk, row0 + (b + 1) * nk)
            dst_ref[dst, :] = jax.nn.gelu(acts[p][src, :]).astype(BF16) * gate.reshape(nk, tm)

    carried = pieces[-1] * nk
    a_ref[0:carried, :] = carry_ref[...]
    for p in range(len(pieces) - 1):
        gated(p, a_ref, carried + key0[p] * nk)
    yt_ref[...] += jnp.dot(wv_ref[...], a_ref[...], preferred_element_type=F32)
    gated(len(pieces) - 1, carry_ref, 0)


def _peer_pieces(keys_per_step):
    if keys_per_step >= 4:
        return (keys_per_step - 2, 1, 1)
    return (keys_per_step - 1, 1)


def _peer_dense(ht, w_u, layer, w_vt_padded, sel, heads, nk):
    D, T = ht.shape
    E = w_u.shape[1]
    tm = min(512, T)
    te = _peer_tile(nk)
    pieces = _peer_pieces(te // nk)
    n_tiles = E // te
    sel_spec = pl.BlockSpec((heads * nk, tm), lambda i, e: (0, i), pipeline_mode=pl.Buffered(1))
    need = _nbytes((tm, D), BF16) + 4 * _nbytes((te, D), BF16) + 2 * _nbytes((D, tm), F32) \
        + 4 * _nbytes((heads * nk, tm), F32) + 2 * _nbytes((te, tm), BF16) + 4 * _nbytes((te, tm), F32)
    return pl.pallas_call(
        functools.partial(_peer_dense_kernel, heads=heads, nk=nk, pieces=pieces, n_tiles=n_tiles),
        out_shape=jax.ShapeDtypeStruct((D, T), F32),
        grid=(T // tm, n_tiles + 1),
        in_specs=[
            pl.BlockSpec((D, tm), lambda i, e: (0, i), pipeline_mode=pl.Buffered(1)),
            pl.BlockSpec((None, te, D), lambda i, e: (layer, jnp.minimum(e, n_tiles - 1), 0)),
            pl.BlockSpec((D, te), lambda i, e: (0, e)),
            sel_spec, sel_spec, sel_spec, sel_spec,
        ],
        out_specs=pl.BlockSpec((D, tm), lambda i, e: (0, i)),
        scratch_shapes=[pltpu.VMEM((te, tm), BF16), pltpu.VMEM((pieces[-1] * nk, tm), BF16)],
        compiler_params=_cparams(("parallel", "arbitrary"), need),
        name="peer_dense",
    )(ht, w_u, w_vt_padded, *sel)


def _peer_tile(nk):
    return min(4, nk) * nk


def _transpose_pad_kernel(w_ref, o_ref, *, n_blocks):
    j = pl.program_id(0)
    inside = (j > 0) & (j <= n_blocks)

    @pl.when(inside)
    def _():
        o_ref[...] = w_ref[...].T.astype(o_ref.dtype)

    @pl.when(jnp.logical_not(inside))
    def _():
        o_ref[...] = jnp.zeros_like(o_ref)


def _transpose_pad(w_v, layer, front, back):
    _, E, D = w_v.shape
    n_blocks = E // front
    assert back % front == 0
    return pl.pallas_call(
        functools.partial(_transpose_pad_kernel, n_blocks=n_blocks),
        out_shape=jax.ShapeDtypeStruct((D, front + E + back), BF16),
        grid=(1 + n_blocks + back // front,),
        in_specs=[pl.BlockSpec((None, front, D), lambda j: (layer, jnp.clip(j - 1, 0, n_blocks - 1), 0))],
        out_specs=pl.BlockSpec((D, front), lambda j: (0, j)),
        compiler_params=_cparams(("parallel",), 6 * _nbytes((front, D), F32)),
        name="transpose_pad_wv",
    )(w_v)


def _mixer_sublayer(h, w_in, layer, cos_t, sin_t, w_gate, b_gate, ln_v_g, ln_v_b, w_sp, b_sp, conv_w,
                    w_branch, w_o, seq):
    mix = ln_v_g.shape[0]
    heads = mix // HEAD_DIM
    proj_ac = _proj_ac(h, w_in, layer, mix)
    qkv = _proj_qkv(h, w_in, layer, cos_t, sin_t, mix)
    branch_a = _gmlp(proj_ac, ln_v_g, ln_v_b, w_sp, b_sp, mix)
    outs, lses = [], []
    for g, (_, dilation) in enumerate(B_PATTERNS):
        o, lse = _dilated_attention(qkv, g, dilation, seq, heads)
        outs.append(o)
        lses.append(lse)
    branch_b = _combine_groups(outs, lses)
    branch_c = _short_conv(proj_ac, conv_w, seq, mix, 2 * mix)
    merged = _merge(h, (branch_a, branch_b, branch_c), w_gate, layer, b_gate, w_branch)
    return _matmul(merged, w_o, layer, BF16, "out_proj")


def _peer_sublayer(h, ht, w_pq, sub_keys, w_u, layer, w_v):
    heads, _, nk, _ = sub_keys.shape
    keys_bd = jax.scipy.linalg.block_diag(
        *[sub_keys[hh, p] for hh in range(heads) for p in range(2)]).astype(BF16)
    te = _peer_tile(nk)
    carried = _peer_pieces(te // nk)[-1] * nk
    w_vt_padded = _transpose_pad(w_v, layer, carried, te - carried)
    st = _peer_scores(h, w_pq.astype(BF16), keys_bd)
    sel = _peer_select(st, heads, nk)
    return _peer_dense(ht, w_u, layer, w_vt_padded, sel, heads, nk)


def kernel(x, c, positions, w_ada, b_ada, w_in, w_gate, b_gate, ln_v_g, ln_v_b, w_sp, b_sp,
           conv_w, w_branch, w_o, ln1_g, ln1_b, w_pq, sub_keys, w_u, w_v, ln2_g, ln2_b):
    B, S, D = x.shape
    depth = w_ada.shape[0]
    alpha = (2 * depth) ** 0.25
    assert S % (ATTN_BLOCK * B_PATTERNS[-1][1]) == 0

    cos_t, sin_t = _rope_tables(positions)
    modr = _adaln(c, w_ada, b_ada)

    def mod_row(layer, k):
        return lambda b: (layer * B + b) * 6 + k

    w_u_b = w_u.astype(BF16)
    w_branch_b = w_branch.astype(BF16)

    h = _modulate(x, modr, mod_row(0, 1), mod_row(0, 0))
    for l in range(depth):
        y = _mixer_sublayer(h, w_in, l, cos_t, sin_t, w_gate, b_gate[l],
                            ln_v_g[l], ln_v_b[l], w_sp[l], b_sp[l], conv_w[l],
                            w_branch_b, w_o, S)
        x, h, ht = _residual_ln(x, y, modr, mod_row(l, 2), ln1_g[l], ln1_b[l], alpha, y_transposed=False,
                                next_rows=(mod_row(l, 4), mod_row(l, 3)), emit_ht=True)
        y_t = _peer_sublayer(h, ht, w_pq[l], sub_keys[l], w_u_b, l, w_v)
        nxt = (mod_row(l + 1, 1), mod_row(l + 1, 0)) if l + 1 < depth else None
        x, h, _ = _residual_ln(x, y_t, modr, mod_row(l, 5), ln2_g[l], ln2_b[l], alpha, y_transposed=True,
                               next_rows=nxt)
    return x
```
